```python
import math
import jax, jax.numpy as jnp
from jax import lax
import numpy as np

D_MODEL = 1024
BATCH = 4
SEQ = 8192
DEPTH = 1

D_MIX = D_MODEL
EPS = 1e-6
ROPE_THETA = 10000.0

MLA_HEADS = 8
MLA_QK_NOPE = 64
MLA_QK_ROPE = 32
MLA_QK_DIM = MLA_QK_NOPE + MLA_QK_ROPE
MLA_V_DIM = 64
MLA_Q_RANK = 256
MLA_KV_RANK = 128
MLA_WIDTH = MLA_HEADS * MLA_V_DIM
ATTN_BLOCK = 128

RET_HEADS = 8
RET_HEAD_DIM = 64
RET_WIDTH = RET_HEADS * RET_HEAD_DIM
RET_CHUNK = 128

IN_SPLITS = (
    MLA_Q_RANK,
    MLA_KV_RANK,
    MLA_QK_ROPE,
    RET_WIDTH, RET_WIDTH, RET_WIDTH, RET_WIDTH,
)
IN_COLS = sum(IN_SPLITS)

PEER_HEADS = 8
PEER_N_KEYS = 128
PEER_N_EXPERTS = PEER_N_KEYS * PEER_N_KEYS
PEER_KEY_DIM = 256
PEER_HALF = PEER_KEY_DIM // 2
PEER_TOPK = 16
PEER_TOKEN_BLOCK = 128

kernel_name = "hymba_mla_retnet_peer_adaln"


def rms_norm(x, g):
    xf = x.astype(jnp.float32)
    y = xf * lax.rsqrt(jnp.mean(xf * xf, axis=-1, keepdims=True) + EPS)
    return (y * g.astype(jnp.float32)).astype(x.dtype)


def rope(x, positions):
    d = x.shape[-1]
    inv = ROPE_THETA ** (-jnp.arange(0, d, 2, dtype=jnp.float32) / d)
    ang = positions.astype(jnp.float32)[:, :, None, None] * inv
    cos, sin = jnp.cos(ang), jnp.sin(ang)
    x1, x2 = jnp.split(x.astype(jnp.float32), 2, axis=-1)
    out = jnp.concatenate([x1 * cos - x2 * sin, x1 * sin + x2 * cos], axis=-1)
    return out.astype(x.dtype)


def mla_group(q_lat, kv_lat, k_rope, positions, g_q_norm, w_uq, g_kv_norm, w_ukv):
    B, S, _ = q_lat.shape
    q = (rms_norm(q_lat, g_q_norm) @ w_uq).reshape(B, S, MLA_HEADS, MLA_QK_DIM)
    q_nope, q_rope = q[..., :MLA_QK_NOPE], q[..., MLA_QK_NOPE:]
    q_rope = rope(q_rope, positions)
    kv = (rms_norm(kv_lat, g_kv_norm) @ w_ukv).reshape(B, S, MLA_HEADS, MLA_QK_NOPE + MLA_V_DIM)
    k_nope, v = kv[..., :MLA_QK_NOPE], kv[..., MLA_QK_NOPE:]
    k_r = rope(k_rope[:, :, None, :], positions)
    k = jnp.concatenate([k_nope, jnp.broadcast_to(k_r, (B, S, MLA_HEADS, MLA_QK_ROPE))], axis=-1)
    qh = jnp.concatenate([q_nope, q_rope], axis=-1) * (MLA_QK_DIM ** -0.5)

    nb = S // ATTN_BLOCK
    q_blocks = qh.reshape(B, nb, ATTN_BLOCK, MLA_HEADS, MLA_QK_DIM).transpose(1, 0, 2, 3, 4)
    key_pos = jnp.arange(S)

    def one_block(args):
        qb, bi = args
        s = jnp.einsum('bqhd,bkhd->bhqk', qb, k).astype(jnp.float32)
        q_pos = bi * ATTN_BLOCK + jnp.arange(ATTN_BLOCK)
        mask = key_pos[None, :] <= q_pos[:, None]
        s = jnp.where(mask[None, None], s, -jnp.inf)
        p = jax.nn.softmax(s, axis=-1).astype(v.dtype)
        return jnp.einsum('bhqk,bkhd->bqhd', p, v)

    out = lax.map(one_block, (q_blocks, jnp.arange(nb)))
    return out.transpose(1, 0, 2, 3, 4).reshape(B, S, MLA_WIDTH)


def retention_group(q, k, v, gate, positions, g_ret_norm):
    B, S, _ = q.shape
    H, d, C = RET_HEADS, RET_HEAD_DIM, RET_CHUNK
    nc = S // C
    q = rope(q.reshape(B, S, H, d), positions)
    k = rope(k.reshape(B, S, H, d), positions) * (d ** -0.5)
    v = v.reshape(B, S, H, d)

    gamma = 1.0 - 2.0 ** (-5.0 - jnp.arange(H, dtype=jnp.float32))
    log_g = jnp.log(gamma)
    idx = jnp.arange(C, dtype=jnp.float32)
    diff = idx[:, None] - idx[None, :]
    dmask = jnp.where(diff >= 0, jnp.exp(log_g[:, None, None] * jnp.maximum(diff, 0.0)), 0.0)

    qc = q.reshape(B, nc, C, H, d).astype(jnp.float32)
    kc = k.reshape(B, nc, C, H, d).astype(jnp.float32)
    vc = v.reshape(B, nc, C, H, d).astype(jnp.float32)

    scores = jnp.einsum('bnihd,bnjhd->bnhij', qc, kc) * dmask
    y_inner = jnp.einsum('bnhij,bnjhd->bnihd', scores, vc)

    zeta = jnp.exp(log_g[:, None] * (C - 1.0 - idx))
    kv_chunk = jnp.einsum('bnjhk,hj,bnjhv->bnhkv', kc, zeta, vc)
    chunk_decay = jnp.exp(log_g * C)

    def step(state, kv_n):
        return state * chunk_decay[None, :, None, None] + kv_n, state

    _, prev = lax.scan(step, jnp.zeros((B, H, d, d), jnp.float32), kv_chunk.transpose(1, 0, 2, 3, 4))
    prev = prev.transpose(1, 0, 2, 3, 4)
    xi = jnp.exp(log_g[:, None] * (idx + 1.0))
    y_cross = jnp.einsum('bnihk,bnhkv,hi->bnihv', qc, prev, xi)

    y = (y_inner + y_cross).reshape(B, S, H, d)
    mu = jnp.mean(y, axis=-1, keepdims=True)
    var = jnp.mean(jnp.square(y - mu), axis=-1, keepdims=True)
    yn = ((y - mu) * lax.rsqrt(var + EPS)).reshape(B, S, RET_WIDTH) * g_ret_norm.astype(jnp.float32)
    return (jax.nn.silu(gate.astype(jnp.float32)) * yn).astype(gate.dtype)


def peer(h, w_query, sub_keys, expert_u, expert_v):
    B, S, D = h.shape
    T = B * S
    K = PEER_TOPK
    xt = h.reshape(T, D)
    q = (xt @ w_query).reshape(T, PEER_HEADS, 2, PEER_HALF)
    s = jnp.einsum('thpd,pkd->thpk', q, sub_keys).astype(jnp.float32)
    s_top, i_top = lax.top_k(s, K)
    cand = (s_top[:, :, 0, :, None] + s_top[:, :, 1, None, :]).reshape(T, PEER_HEADS, K * K)
    cand_idx = (i_top[:, :, 0, :, None] * PEER_N_KEYS + i_top[:, :, 1, None, :]).reshape(T, PEER_HEADS, K * K)
    best, pos = lax.top_k(cand, K)
    eidx = jnp.take_along_axis(cand_idx, pos, axis=-1)
    g = jax.nn.softmax(best, axis=-1)

    nb = T // PEER_TOKEN_BLOCK

    def one_block(args):
        xb, ib, gb = args
        u = expert_u[ib]
        a = jax.nn.gelu(jnp.einsum('thkd,td->thk', u, xb).astype(jnp.float32), approximate=False)
        w = (gb * a).astype(expert_v.dtype)
        return jnp.einsum('thk,thkd->td', w, expert_v[ib])

    out = lax.map(one_block, (xt.reshape(nb, PEER_TOKEN_BLOCK, D),
                              eidx.reshape(nb, PEER_TOKEN_BLOCK, PEER_HEADS, K),
                              g.reshape(nb, PEER_TOKEN_BLOCK, PEER_HEADS, K)))
    return out.reshape(B, S, D)


def setup_inputs(seed: int = 0) -> dict:
    key = jax.random.key(seed)
    ks = jax.random.split(key, 20)
    L, D = DEPTH, D_MODEL
    nrm = lambda k, shape, scale: jax.random.normal(k, shape, jnp.float32) * scale
    gain = lambda k, shape: 1.0 + 0.02 * jax.random.normal(k, shape, jnp.float32)
    return {
        "x": nrm(ks[0], (BATCH, SEQ, D), 1.0),
        "c": nrm(ks[1], (BATCH, D), 1.0),
        "positions": jnp.broadcast_to(jnp.arange(SEQ, dtype=jnp.int32), (BATCH, SEQ)),
        "w_ada": nrm(ks[2], (L, D, 6 * D), D ** -0.5),
        "b_ada": nrm(ks[3], (L, 6 * D), 0.01),
        "g_norm1": gain(ks[4], (L, D)),
        "w_in": nrm(ks[5], (L, D, IN_COLS), D ** -0.5),
        "g_q_norm": gain(ks[6], (L, MLA_Q_RANK)),
        "w_uq": nrm(ks[7], (L, MLA_Q_RANK, MLA_HEADS * MLA_QK_DIM), MLA_Q_RANK ** -0.5),
        "g_kv_norm": gain(ks[8], (L, MLA_KV_RANK)),
        "w_ukv": nrm(ks[9], (L, MLA_KV_RANK, MLA_HEADS * (MLA_QK_NOPE + MLA_V_DIM)), MLA_KV_RANK ** -0.5),
        "g_ret_norm": gain(ks[10], (L, RET_WIDTH)),
        "w_out": nrm(ks[11], (L, MLA_WIDTH + RET_WIDTH, D), (MLA_WIDTH + RET_WIDTH) ** -0.5),
        "g_norm2": gain(ks[12], (L, D)),
        "w_query": nrm(ks[13], (L, D, PEER_HEADS * PEER_KEY_DIM), D ** -0.5),
        "sub_keys": nrm(ks[14], (L, 2, PEER_N_KEYS, PEER_HALF), PEER_HALF ** -0.5),
        "expert_u": nrm(ks[15], (L, PEER_N_EXPERTS, D), D ** -0.5),
        "expert_v": nrm(ks[16], (L, PEER_N_EXPERTS, D), 0.5),
        "g_final": gain(ks[17], (D,)),
    }


def reference(x, c, positions, w_ada, b_ada, g_norm1, w_in, g_q_norm, w_uq, g_kv_norm, w_ukv,
              g_ret_norm, w_out, g_norm2, w_query, sub_keys, expert_u, expert_v, g_final):
    offs = np.cumsum(IN_SPLITS)[:-1].tolist()
    for l in range(DEPTH):
        mod = jax.nn.silu(c) @ w_ada[l] + b_ada[l]
        sh1, sc1, gt1, sh2, sc2, gt2 = [m[:, None, :] for m in jnp.split(mod, 6, axis=-1)]

        h = rms_norm(x, g_norm1[l]) * (1.0 + sc1) + sh1
        proj = h @ w_in[l]
        q_lat, kv_lat, k_rope, rq, rk, rv, rg = jnp.split(proj, offs, axis=-1)
        y_mla = mla_group(q_lat, kv_lat, k_rope, positions, g_q_norm[l], w_uq[l], g_kv_norm[l], w_ukv[l])
        y_ret = retention_group(rq, rk, rv, rg, positions, g_ret_norm[l])
        mixed = jnp.concatenate([y_mla.astype(x.dtype), y_ret.astype(x.dtype)], axis=-1) @ w_out[l]
        x = x + gt1 * mixed

        h2 = rms_norm(x, g_norm2[l]) * (1.0 + sc2) + sh2
        x = x + gt2 * peer(h2, w_query[l], sub_keys[l], expert_u[l], expert_v[l]).astype(x.dtype)
    return rms_norm(x, g_final)
```

```python
import functools
import math

import jax
import jax.numpy as jnp
import numpy as np
from jax import lax
from jax.experimental import pallas as pl
from jax.experimental.pallas import tpu as pltpu

F32 = jnp.float32
BF16 = jnp.bfloat16
HIGHEST = lax.Precision.HIGHEST

EPS = 1e-6
ROPE_THETA = 10000.0

MLA_HEADS = 8
MLA_QK_NOPE = 64
MLA_QK_ROPE = 32
MLA_QK_DIM = MLA_QK_NOPE + MLA_QK_ROPE
MLA_V_DIM = 64
MLA_Q_RANK = 256
MLA_KV_RANK = 128

RET_HEADS = 8
RET_HEAD_DIM = 64
RET_WIDTH = RET_HEADS * RET_HEAD_DIM
RET_CHUNK = 128

PEER_HEADS = 8
PEER_N_KEYS = 128
PEER_HALF = 128
PEER_TOPK = 16
PEER_SLOTS = PEER_HEADS * PEER_TOPK

LANES = 128
VMEM_LIMIT = 56 * 1024 * 1024

_NT = (((1,), (1,)), ((), ()))


def _cparams(sem):
    return pltpu.CompilerParams(dimension_semantics=sem, vmem_limit_bytes=VMEM_LIMIT)


def _adaln_kernel(c_ref, w_ref, b_ref, o_ref):
    c = c_ref[...]
    s = c * jax.nn.sigmoid(c)
    o_ref[...] = jnp.dot(s, w_ref[...], precision=HIGHEST, preferred_element_type=F32) + b_ref[...]


def _adaln(c, w, b):
    bsz, d = c.shape
    n = w.shape[1]
    return pl.pallas_call(
        _adaln_kernel,
        grid=(n // d,),
        in_specs=[
            pl.BlockSpec((bsz, d), lambda j: (0, 0)),
            pl.BlockSpec((d, d), lambda j: (0, j)),
            pl.BlockSpec((1, d), lambda j: (0, j)),
        ],
        out_specs=pl.BlockSpec((bsz, d), lambda j: (0, j)),
        out_shape=jax.ShapeDtypeStruct((bsz, n), F32),
        compiler_params=_cparams(("arbitrary",)),
        name="adaln",
    )(c, w, b.reshape(1, n))


def _rope_tab_kernel(pos_ref, inv_ref, sg_ref, cr_ref, sr_ref, cm_ref, sm_ref):
    pos = pos_ref[...]
    ang_r = pos * inv_ref[0:1, :]
    ang_m = pos * inv_ref[1:2, :]
    cr_ref[...] = jnp.cos(ang_r)
    sr_ref[...] = jnp.sin(ang_r) * sg_ref[0:1, :]
    cm_ref[...] = jnp.cos(ang_m)
    sm_ref[...] = jnp.sin(ang_m) * sg_ref[1:2, :]


def _rope_tables(positions):
    t = positions.size
    pos = positions.reshape(t, 1).astype(F32)
    inv_r = ROPE_THETA ** (-jnp.arange(0, RET_HEAD_DIM, 2, dtype=F32) / RET_HEAD_DIM)
    inv_m = ROPE_THETA ** (-jnp.arange(0, MLA_QK_ROPE, 2, dtype=F32) / MLA_QK_ROPE)
    z = lambda n: jnp.zeros((n,), F32)
    o = lambda n: jnp.ones((n,), F32)
    inv = jnp.stack([
        jnp.tile(inv_r, 4),
        jnp.concatenate([inv_m, z(48), inv_m, z(48)]),
    ])
    sg = jnp.stack([
        jnp.concatenate([-o(64), o(64)]),
        jnp.concatenate([-o(16), z(48), o(16), z(48)]),
    ])
    tm = 512
    tab = jax.ShapeDtypeStruct((t, LANES), F32)
    spec = pl.BlockSpec((tm, LANES), lambda i: (i, 0))
    cst = pl.BlockSpec((2, LANES), lambda i: (0, 0))
    return pl.pallas_call(
        _rope_tab_kernel,
        grid=(t // tm,),
        in_specs=[pl.BlockSpec((tm, 1), lambda i: (i, 0)), cst, cst],
        out_specs=[spec] * 4,
        out_shape=[tab] * 4,
        compiler_params=_cparams(("parallel",)),
        name="rope_tables",
    )(pos, inv, sg)


def _rms(x, g):
    return x * lax.rsqrt(jnp.mean(x * x, axis=-1, keepdims=True) + EPS) * g


def _rot(x, c, s):
    return x * c + pltpu.roll(x, 64, 1) * s


def _proj_kernel(x_ref, sc_ref, sh_ref, g1_ref, wa_ref, wr_ref, gq_ref, wuq_ref, gkv_ref,
                 wuk_ref, wuv_ref, cr_ref, sr_ref, cm_ref, sm_ref,
                 q_ref, k_ref, v_ref, rq_ref, rk_ref, rv_ref, rg_ref):
    x = x_ref[...]
    h = _rms(x, g1_ref[...]) * (1.0 + sc_ref[...]) + sh_ref[...]
    hb = h.astype(BF16)
    cm, sm = cm_ref[...], sm_ref[...]
    cr, sr = cr_ref[...], sr_ref[...]

    pa = jnp.dot(hb, wa_ref[...], preferred_element_type=F32)
    q_lat = pa[:, :MLA_Q_RANK]
    kv_lat = pa[:, MLA_Q_RANK:MLA_Q_RANK + MLA_KV_RANK]
    kr = _rot(pa[:, MLA_Q_RANK + MLA_KV_RANK:], cm, sm)

    qn = _rms(q_lat, gq_ref[...]).astype(BF16)
    q = jnp.dot(qn, wuq_ref[...], preferred_element_type=F32)
    scale = MLA_QK_DIM ** -0.5
    for hd in range(MLA_HEADS):
        sl = slice(hd * LANES, (hd + 1) * LANES)
        q_ref[:, sl] = (_rot(q[:, sl], cm, sm) * scale).astype(BF16)

    kvn = _rms(kv_lat, gkv_ref[...]).astype(BF16)
    k = jnp.dot(kvn, wuk_ref[...], preferred_element_type=F32)
    for hd in range(MLA_HEADS):
        sl = slice(hd * LANES, (hd + 1) * LANES)
        k_ref[:, sl] = (k[:, sl] + kr).astype(BF16)
    v_ref[...] = jnp.dot(kvn, wuv_ref[...], preferred_element_type=F32).astype(BF16)

    pr = jnp.dot(hb, wr_ref[...], preferred_element_type=F32)
    w = RET_WIDTH
    for p in range(RET_HEADS // 2):
        sl = slice(p * LANES, (p + 1) * LANES)
        rq_ref[:, sl] = _rot(pr[:, p * LANES:(p + 1) * LANES], cr, sr).astype(BF16)
        rk_ref[:, sl] = (_rot(pr[:, w + p * LANES:w + (p + 1) * LANES], cr, sr)
                         * (RET_HEAD_DIM ** -0.5)).astype(BF16)
    rv_ref[...] = pr[:, 2 * w:3 * w].astype(BF16)
    rg_ref[...] = pr[:, 3 * w:]


def _proj(x2, sc1, sh1, g1, wa, wr, gq, wuq, gkv, wuk, wuv, tabs, seq):
    t, d = x2.shape
    tm = 256
    tpb = seq // tm
    cr, sr, cm, sm = tabs
    row = lambda n: pl.BlockSpec((tm, n), lambda i: (i, 0))
    full = lambda a: pl.BlockSpec(a.shape, lambda i: (0,) * a.ndim)
    mod = pl.BlockSpec((None, 1, d), lambda i: (i // tpb, 0, 0))
    outs = [(8 * LANES, BF16), (8 * LANES, BF16), (512, BF16), (512, BF16), (512, BF16),
            (512, BF16), (512, F32)]
    return pl.pallas_call(
        _proj_kernel,
        grid=(t // tm,),
        in_specs=[row(d), mod, mod, full(g1), full(wa), full(wr), full(gq), full(wuq), full(gkv),
                  full(wuk), full(wuv), row(LANES), row(LANES), row(LANES), row(LANES)],
        out_specs=[row(n) for n, _ in outs],
        out_shape=[jax.ShapeDtypeStruct((t, n), dt) for n, dt in outs],
        compiler_params=_cparams(("parallel",)),
        name="in_proj",
    )(x2, sc1, sh1, g1, wa, wr, gq, wuq, gkv, wuk, wuv, cr, sr, cm, sm)


def _attn_kernel(q_ref, k_ref, v_ref, o_ref, *, tq, tk):
    i = pl.program_id(2)
    lane = lax.broadcasted_iota(jnp.int32, (tq, LANES), 1)
    row = lax.broadcasted_iota(jnp.int32, (tq, tk), 0)
    col = lax.broadcasted_iota(jnp.int32, (tq, tk), 1)
    outs = []
    for hh in range(2):
        hs = slice(hh * LANES, (hh + 1) * LANES)
        q = q_ref[:, hs]

        def step(off, carry, mask, hs=hs, q=q):
            m, l, acc = carry
            kb = k_ref[pl.ds(off, tk), hs]
            vb = v_ref[pl.ds(off, tk), :]
            s = lax.dot_general(q, kb, _NT, preferred_element_type=F32)
            if mask is not None:
                s = jnp.where(mask, s, -1e30)
            m_new = jnp.maximum(m, jnp.max(s, axis=1, keepdims=True))
            alpha = jnp.exp(m - m_new)
            p = jnp.exp(s - m_new)
            l = alpha * l + jnp.sum(p, axis=1, keepdims=True)
            acc = alpha * acc + jnp.dot(p.astype(BF16), vb, preferred_element_type=F32)
            return m_new, l, acc

        carry = (jnp.full((tq, 1), -1e30, F32), jnp.zeros((tq, 1), F32),
                 jnp.zeros((tq, LANES), F32))
        carry = lax.fori_loop(
            0, i * (tq // tk),
            lambda j, c, step=step: step(pl.multiple_of(j * tk, tk), c, None), carry)
        for dblk in range(tq // tk):
            off = pl.multiple_of(i * tq + dblk * tk, tk)
            carry = step(off, carry, col + dblk * tk <= row)
        _, l, acc = carry
        outs.append(acc / l)
    o_ref[...] = jnp.where(lane < MLA_V_DIM, outs[0], outs[1]).astype(o_ref.dtype)


def _attention(q, k, v):
    b, s, _ = q.shape
    tq = tk = 512
    return pl.pallas_call(
        functools.partial(_attn_kernel, tq=tq, tk=tk),
        grid=(b, MLA_HEADS // 2, s // tq),
        in_specs=[
            pl.BlockSpec((None, tq, 2 * LANES), lambda bi, p, i: (bi, i, p)),
            pl.BlockSpec((None, s, 2 * LANES), lambda bi, p, i: (bi, 0, p)),
            pl.BlockSpec((None, s, LANES), lambda bi, p, i: (bi, 0, p)),
        ],
        out_specs=pl.BlockSpec((None, tq, LANES), lambda bi, p, i: (bi, i, p)),
        out_shape=jax.ShapeDtypeStruct((b, s, MLA_HEADS * MLA_V_DIM), BF16),
        compiler_params=_cparams(("parallel", "parallel", "arbitrary")),
        name="mla_attention",
    )(q, k, v)


def _ret_kernel(q_ref, k_ref, v_ref, g_ref, dm_ref, z_ref, xi_ref, dec_ref, gn_ref, o_ref, st_ref,
                *, nchunk):
    c = RET_CHUNK

    @pl.when(pl.program_id(2) == 0)
    def _():
        st_ref[...] = jnp.zeros_like(st_ref)

    lane = lax.broadcasted_iota(jnp.int32, (c, LANES), 1)
    sub = lax.broadcasted_iota(jnp.int32, (c, LANES), 0)
    v_first = lane < RET_HEAD_DIM
    k_first = (lane & 32) == 0
    same_head = ((sub & 32) == 0) == v_first
    for ci in range(nchunk):
        sl = slice(ci * c, (ci + 1) * c)
        q, k, v = q_ref[sl, :], k_ref[sl, :], v_ref[sl, :]
        zero = jnp.zeros_like(q)
        s_a = lax.dot_general(jnp.where(k_first, q, zero), k, _NT, preferred_element_type=F32) * dm_ref[0]
        s_b = lax.dot_general(jnp.where(k_first, zero, q), k, _NT, preferred_element_type=F32) * dm_ref[1]
        s_ab = jnp.concatenate([s_a, s_b], axis=1).astype(BF16)
        v_bd = jnp.concatenate([jnp.where(v_first, v, zero), jnp.where(v_first, zero, v)], axis=0)
        y = jnp.dot(s_ab, v_bd, preferred_element_type=F32)

        st = st_ref[...]
        st_hi = st.astype(BF16)
        st_lo = (st - st_hi.astype(F32)).astype(BF16)
        y = y + (jnp.dot(q, st_hi, preferred_element_type=F32)
                 + jnp.dot(q, st_lo, preferred_element_type=F32)) * xi_ref[...]

        vz = (v.astype(F32) * z_ref[...]).astype(BF16)
        kt = k.astype(F32).T.astype(BF16)
        kv = jnp.dot(kt, vz, preferred_element_type=F32)
        st_ref[...] = st * dec_ref[...] + jnp.where(same_head, kv, 0.0)

        def head_mean(a):
            tot = jnp.sum(a, axis=1, keepdims=True)
            first = jnp.sum(jnp.where(v_first, a, 0.0), axis=1, keepdims=True)
            return jnp.where(v_first, first, tot - first) * (1.0 / RET_HEAD_DIM)

        dlt = y - head_mean(y)
        yn = dlt * lax.rsqrt(head_mean(dlt * dlt) + EPS) * gn_ref[...]
        gate = g_ref[sl, :]
        o_ref[sl, :] = (gate * jax.nn.sigmoid(gate) * yn).astype(o_ref.dtype)


def _retention_consts():
    h, c = RET_HEADS, RET_CHUNK
    gamma = 1.0 - 2.0 ** (-5.0 - jnp.arange(h, dtype=F32))
    log_g = jnp.log(gamma)
    idx = jnp.arange(c, dtype=F32)
    diff = idx[:, None] - idx[None, :]
    dmask = jnp.where(diff >= 0, jnp.exp(log_g[:, None, None] * jnp.maximum(diff, 0.0)), 0.0)
    zeta = jnp.exp(log_g[:, None] * (c - 1.0 - idx))
    xi = jnp.exp(log_g[:, None] * (idx + 1.0))
    decay = jnp.exp(log_g * c)
    by_lane = lambda a: jnp.repeat(a.reshape(h // 2, 2, -1), RET_HEAD_DIM, axis=1)
    z = by_lane(zeta).transpose(0, 2, 1)
    x = by_lane(xi).transpose(0, 2, 1)
    dec = by_lane(decay[:, None]).transpose(0, 2, 1)
    return dmask, z, x, dec


def _retention(rq, rk, rv, rg, g_ret):
    b, s, w = rq.shape
    tc = 512
    dmask, z, xi, dec = _retention_consts()
    blk = pl.BlockSpec((None, tc, LANES), lambda bi, p, t: (bi, t, p))
    per_pair = lambda shp: pl.BlockSpec((None,) + shp, lambda bi, p, t: (p, 0, 0))
    return pl.pallas_call(
        functools.partial(_ret_kernel, nchunk=tc // RET_CHUNK),
        grid=(b, RET_HEADS // 2, s // tc),
        in_specs=[blk, blk, blk, blk,
                  pl.BlockSpec((2, RET_CHUNK, RET_CHUNK), lambda bi, p, t: (p, 0, 0)),
                  per_pair((RET_CHUNK, LANES)), per_pair((RET_CHUNK, LANES)), per_pair((1, LANES)),
                  pl.BlockSpec((1, LANES), lambda bi, p, t: (0, p))],
        out_specs=blk,
        out_shape=jax.ShapeDtypeStruct((b, s, w), BF16),
        scratch_shapes=[pltpu.VMEM((LANES, LANES), F32)],
        compiler_params=_cparams(("parallel", "parallel", "arbitrary")),
        name="retention",
    )(rq, rk, rv, rg, dmask, z, xi, dec, g_ret.reshape(1, w))


def _mix_kernel(ym_ref, yr_ref, x_ref, gt_ref, sc_ref, sh_ref, g2_ref, wo_ref, wq_ref, keys_ref,
                x1_ref, h2_ref, st_ref):
    half = ym_ref.shape[1]
    mixed = (jnp.dot(ym_ref[...], wo_ref[:half, :], preferred_element_type=F32)
             + jnp.dot(yr_ref[...], wo_ref[half:, :], preferred_element_type=F32))
    x1 = x_ref[...] + gt_ref[...] * mixed
    x1_ref[...] = x1
    h2 = _rms(x1, g2_ref[...]) * (1.0 + sc_ref[...]) + sh_ref[...]
    h2_ref[...] = h2
    pq = jnp.dot(h2.astype(BF16), wq_ref[...], preferred_element_type=F32)
    for g in range(2 * PEER_HEADS):
        qg = pq[:, g * PEER_HALF:(g + 1) * PEER_HALF].astype(BF16)
        st_ref[g * PEER_N_KEYS:(g + 1) * PEER_N_KEYS, :] = lax.dot_general(
            keys_ref[g % 2], qg, _NT, preferred_element_type=F32)


def _mix(ym, yr, x2, gt1, sc2, sh2, g2, wo, wq, keys, seq):
    t, d = x2.shape
    tm = 256
    tpb = seq // tm
    row = lambda n: pl.BlockSpec((tm, n), lambda i: (i, 0))
    full = lambda a: pl.BlockSpec(a.shape, lambda i: (0,) * a.ndim)
    mod = pl.BlockSpec((None, 1, d), lambda i: (i // tpb, 0, 0))
    ns = 2 * PEER_HEADS * PEER_N_KEYS
    return pl.pallas_call(
        _mix_kernel,
        grid=(t // tm,),
        in_specs=[row(ym.shape[1]), row(yr.shape[1]), row(d), mod, mod, mod, full(g2), full(wo),
                  full(wq), full(keys)],
        out_specs=[row(d), row(d), pl.BlockSpec((ns, tm), lambda i: (0, i))],
        out_shape=[jax.ShapeDtypeStruct((t, d), F32), jax.ShapeDtypeStruct((t, d), F32),
                   jax.ShapeDtypeStruct((ns, t), F32)],
        compiler_params=_cparams(("parallel",)),
        name="out_proj_peer_scores",
    )(ym, yr, x2, gt1, sc2, sh2, g2, wo, wq, keys)


def _top16(s, payload=None):
    n = s.shape[0]
    rows = lax.broadcasted_iota(jnp.int32, s.shape, 0)
    vals, sel = [], []
    for _ in range(PEER_TOPK):
        m = jnp.max(s, axis=0, keepdims=True)
        at = jnp.min(jnp.where(s == m, rows, n), axis=0, keepdims=True)
        hit = rows == at
        vals.append(m)
        sel.append(at if payload is None else jnp.max(jnp.where(hit, payload, -1), axis=0, keepdims=True))
        s = jnp.where(hit, -jnp.inf, s)
    return jnp.concatenate(vals, axis=0), jnp.concatenate(sel, axis=0)


def _topk_kernel(st_ref, e_ref, g_ref, es_ref, gs_ref):
    nk, k = PEER_N_KEYS, PEER_TOPK

    def head(h, _):
        base = pl.multiple_of(h * 2 * nk, 2 * nk)
        v0, i0 = _top16(st_ref[pl.ds(base, nk), :])
        v1, i1 = _top16(st_ref[pl.ds(base + nk, nk), :])
        cand = jnp.concatenate([v0[a:a + 1, :] + v1 for a in range(k)], axis=0)
        cidx = jnp.concatenate([i0[a:a + 1, :] * nk + i1 for a in range(k)], axis=0)
        best, eidx = _top16(cand, cidx)
        ex = jnp.exp(best - jnp.max(best, axis=0, keepdims=True))
        gate = ex / jnp.sum(ex, axis=0, keepdims=True)
        row = pl.multiple_of(h * k, k)
        es_ref[pl.ds(row, k), :] = eidx
        gs_ref[pl.ds(row, k), :] = gate
        return 0

    lax.fori_loop(0, PEER_HEADS, head, 0)
    e_ref[...] = es_ref[...].T * 4
    g_ref[...] = gs_ref[...].T


def _topk(st):
    ns, t = st.shape
    tt = 256
    out = pl.BlockSpec((tt, PEER_SLOTS), lambda i: (i, 0))
    return pl.pallas_call(
        _topk_kernel,
        grid=(t // tt,),
        in_specs=[pl.BlockSpec((ns, tt), lambda i: (0, i))],
        out_specs=[out, out],
        out_shape=[jax.ShapeDtypeStruct((t, PEER_SLOTS), jnp.int32),
                   jax.ShapeDtypeStruct((t, PEER_SLOTS), F32)],
        scratch_shapes=[pltpu.VMEM((PEER_SLOTS, tt), jnp.int32), pltpu.VMEM((PEER_SLOTS, tt), F32)],
        compiler_params=_cparams(("parallel",)),
        name="peer_topk",
    )(st)


PEER_TB = 128
ROWS_PER_EXPERT = 4
TILE_ROWS = PEER_SLOTS * ROWS_PER_EXPERT


def _pack_table(tab):
    e, d = tab.shape
    tb = lax.bitcast_convert_type(tab.astype(BF16), jnp.uint16).astype(jnp.uint32)
    tb = tb.reshape(e, 2, ROWS_PER_EXPERT, LANES)
    word = tb[:, 0] | (tb[:, 1] << 16)
    return lax.bitcast_convert_type(word, jnp.int32).reshape(e * ROWS_PER_EXPERT, LANES)


def _peer_layout():
    j = np.arange(2 * TILE_ROWS)
    chunk = (j % 8) // 2 + 4 * (j % 2)
    mask8 = (chunk[None, :] == np.arange(8)[:, None]).astype(np.float32)
    group = (j[:, None] // 8 == np.arange(PEER_SLOTS)[None, :]).astype(np.float32)
    return jnp.asarray(mask8), jnp.asarray(group), jnp.asarray(group.T)


def _load_table(tab_hbm, tab_vmem, sem):
    @pl.when(pl.program_id(0) == 0)
    def _():
        cp = pltpu.make_async_copy(tab_hbm, tab_vmem, sem)
        cp.start()
        cp.wait()


def _gather_tile(idx_ref, tab_ref, tile_ref, token):
    base = token * PEER_SLOTS
    for s in range(PEER_SLOTS):
        row = pl.multiple_of(idx_ref[base + s], ROWS_PER_EXPERT)
        tile_ref[pl.ds(s * ROWS_PER_EXPERT, ROWS_PER_EXPERT), :] = tab_ref[pl.ds(row, ROWS_PER_EXPERT), :]
    return pltpu.bitcast(tile_ref[...], BF16)


def _split_bf16(a):
    hi = a.astype(BF16)
    lo = (a - hi.astype(F32)).astype(BF16)
    return jnp.concatenate([hi, lo], axis=0)


def _peer_u_kernel(idx_ref, x_ref, mask_ref, tab_hbm, o_ref, tab_ref, tile_a, tile_b, sem):
    _load_table(tab_hbm, tab_ref, sem)
    mask8 = mask_ref[...]

    def group(t8, _):
        rows = []
        for j in range(8):
            tok = t8 * 8 + j
            tile = _gather_tile(idx_ref, tab_ref, tile_a if j % 2 == 0 else tile_b, tok)
            d = lax.dot_general(_split_bf16(x_ref[tok]), tile, _NT, preferred_element_type=F32)
            rows.append(jnp.sum((d[:8] + d[8:]) * mask8, axis=0, keepdims=True))
        o_ref[pl.ds(pl.multiple_of(t8 * 8, 8), 8), :] = jnp.concatenate(rows, axis=0)
        return 0

    lax.fori_loop(0, PEER_TB // 8, group, 0)


def _peer_v_kernel(idx_ref, a_ref, g_ref, mask_ref, grp_ref, grpt_ref, tab_hbm, o_ref,
                   tab_ref, tile_a, tile_b, wx_ref, sem):
    _load_table(tab_hbm, tab_ref, sem)
    mask8 = mask_ref[...]
    act = jnp.dot(a_ref[...], grp_ref[...], precision=HIGHEST, preferred_element_type=F32)
    gelu = 0.5 * act * (1.0 + lax.erf(act * math.sqrt(0.5)))
    w = g_ref[...] * gelu
    wx_ref[...] = jnp.dot(w, grpt_ref[...], precision=HIGHEST, preferred_element_type=F32)

    def group(t8, _):
        w8 = wx_ref[pl.ds(pl.multiple_of(t8 * 8, 8), 8), :]
        for j in range(8):
            tok = t8 * 8 + j
            tile = _gather_tile(idx_ref, tab_ref, tile_a if j % 2 == 0 else tile_b, tok)
            lhs = _split_bf16(w8[j:j + 1, :] * mask8)
            out = jnp.dot(lhs, tile, preferred_element_type=F32)
            o_ref[tok] = out[:8] + out[8:]
        return 0

    lax.fori_loop(0, PEER_TB // 8, group, 0)


def _peer_scratch():
    n_rows = PEER_N_KEYS * PEER_N_KEYS * ROWS_PER_EXPERT
    return [pltpu.VMEM((n_rows, LANES), jnp.int32), pltpu.VMEM((TILE_ROWS, LANES), jnp.int32),
            pltpu.VMEM((TILE_ROWS, LANES), jnp.int32)]


def _peer_apply(h2, eidx4, gate, tab_u, tab_v):
    t, d = h2.shape
    tb = PEER_TB
    mask8, grp, grpt = _peer_layout()
    idx = eidx4.reshape(t * PEER_SLOTS)
    idx_spec = pl.BlockSpec((tb * PEER_SLOTS,), lambda i: (i,), memory_space=pltpu.SMEM)
    full = lambda a: pl.BlockSpec(a.shape, lambda i: (0,) * a.ndim)
    tok3 = pl.BlockSpec((tb, 8, LANES), lambda i: (i, 0, 0))
    hbm = pl.BlockSpec(memory_space=pl.ANY)
    act = pl.pallas_call(
        _peer_u_kernel,
        grid=(t // tb,),
        in_specs=[idx_spec, tok3, full(mask8), hbm],
        out_specs=pl.BlockSpec((tb, 2 * TILE_ROWS), lambda i: (i, 0)),
        out_shape=jax.ShapeDtypeStruct((t, 2 * TILE_ROWS), F32),
        scratch_shapes=_peer_scratch() + [pltpu.SemaphoreType.DMA(())],
        compiler_params=_cparams(("arbitrary",)),
        name="peer_u",
    )(idx, h2.reshape(t, 8, LANES), mask8, tab_u)
    out = pl.pallas_call(
        _peer_v_kernel,
        grid=(t // tb,),
        in_specs=[idx_spec, pl.BlockSpec((tb, 2 * TILE_ROWS), lambda i: (i, 0)),
                  pl.BlockSpec((tb, PEER_SLOTS), lambda i: (i, 0)), full(mask8), full(grp), full(grpt), hbm],
        out_specs=tok3,
        out_shape=jax.ShapeDtypeStruct((t, 8, LANES), F32),
        scratch_shapes=_peer_scratch() + [pltpu.VMEM((tb, 2 * TILE_ROWS), F32), pltpu.SemaphoreType.DMA(())],
        compiler_params=_cparams(("arbitrary",)),
        name="peer_v",
    )(idx, act, gate, mask8, grp, grpt, tab_v)
    return out.reshape(t, d)


def _resid_kernel(x_ref, p_ref, gt_ref, o_ref):
    o_ref[...] = x_ref[...] + gt_ref[...] * p_ref[...]


def _final_kernel(x_ref, g_ref, o_ref):
    o_ref[...] = _rms(x_ref[...], g_ref[...])


def _residual(x1, peer, gt2, seq):
    t, d = x1.shape
    tm = 512
    tpb = seq // tm
    row = pl.BlockSpec((tm, d), lambda i: (i, 0))
    return pl.pallas_call(
        _resid_kernel,
        grid=(t // tm,),
        in_specs=[row, row, pl.BlockSpec((None, 1, d), lambda i: (i // tpb, 0, 0))],
        out_specs=row,
        out_shape=jax.ShapeDtypeStruct((t, d), F32),
        compiler_params=_cparams(("parallel",)),
        name="peer_residual",
    )(x1, peer, gt2)


def _final_norm(x2, g):
    t, d = x2.shape
    tm = 512
    row = pl.BlockSpec((tm, d), lambda i: (i, 0))
    return pl.pallas_call(
        _final_kernel,
        grid=(t // tm,),
        in_specs=[row, pl.BlockSpec((1, d), lambda i: (0, 0))],
        out_specs=row,
        out_shape=jax.ShapeDtypeStruct((t, d), F32),
        compiler_params=_cparams(("parallel",)),
        name="final_norm",
    )(x2, g.reshape(1, d))


def _mla_head_cols(rope_cols, nope_cols):
    pad = lambda n: [-1] * n
    r1 = list(rope_cols[:16]) if rope_cols is not None else pad(16)
    r2 = list(rope_cols[16:]) if rope_cols is not None else pad(16)
    n1 = list(nope_cols[:48]) if nope_cols is not None else pad(48)
    n2 = list(nope_cols[48:]) if nope_cols is not None else pad(16)
    return r1 + n1 + r2 + n2 + pad(32)


def _take_cols(w, cols):
    cols = np.asarray(cols)
    out = jnp.take(w, jnp.asarray(np.maximum(cols, 0)), axis=1)
    return jnp.where(jnp.asarray(cols >= 0)[None, :], out, 0.0)


def _layer_weights(w_in, w_uq, w_ukv):
    qk = MLA_QK_DIM
    uq_cols, uk_cols, uv_cols = [], [], []
    for h in range(MLA_HEADS):
        uq_cols += _mla_head_cols(range(h * qk + MLA_QK_NOPE, (h + 1) * qk), range(h * qk, h * qk + MLA_QK_NOPE))
        kv0 = h * (MLA_QK_NOPE + MLA_V_DIM)
        uk_cols += _mla_head_cols(None, range(kv0, kv0 + MLA_QK_NOPE))
        uv_cols += list(range(kv0 + MLA_QK_NOPE, kv0 + MLA_QK_NOPE + MLA_V_DIM))
    o_kr = MLA_Q_RANK + MLA_KV_RANK
    o_r = o_kr + MLA_QK_ROPE
    a_cols = list(range(o_kr)) + _mla_head_cols(range(o_kr, o_r), None)

    def pair_cols(base):
        cols = []
        for p in range(RET_HEADS // 2):
            a, b = base + 2 * p * RET_HEAD_DIM, base + (2 * p + 1) * RET_HEAD_DIM
            cols += list(range(a, a + 32)) + list(range(b, b + 32)) + list(range(a + 32, a + 64)) + list(range(b + 32, b + 64))
        return cols

    r_cols = (pair_cols(o_r) + pair_cols(o_r + RET_WIDTH)
              + list(range(o_r + 2 * RET_WIDTH, o_r + 4 * RET_WIDTH)))
    bf = lambda a: a.astype(BF16)
    return (bf(_take_cols(w_in, a_cols)), bf(_take_cols(w_in, r_cols)), bf(_take_cols(w_uq, uq_cols)),
            bf(_take_cols(w_ukv, uk_cols)), bf(_take_cols(w_ukv, uv_cols)))


def kernel(x, c, positions, w_ada, b_ada, g_norm1, w_in, g_q_norm, w_uq, g_kv_norm, w_ukv, g_ret_norm,
           w_out, g_norm2, w_query, sub_keys, expert_u, expert_v, g_final):
    b, s, d = x.shape
    t = b * s
    depth = w_ada.shape[0]
    tabs = _rope_tables(positions)
    x2 = x.reshape(t, d)
    for l in range(depth):
        mod = _adaln(c, w_ada[l], b_ada[l])
        sh1, sc1, gt1, sh2, sc2, gt2 = [m.reshape(b, 1, d) for m in jnp.split(mod, 6, axis=-1)]
        wa, wr, wuq, wuk, wuv = _layer_weights(w_in[l], w_uq[l], w_ukv[l])
        q, k, v, rq, rk, rv, rg = _proj(
            x2, sc1, sh1, g_norm1[l].reshape(1, d), wa, wr, g_q_norm[l].reshape(1, -1), wuq,
            g_kv_norm[l].reshape(1, -1), wuk, wuv, tabs, s)
        r3 = lambda a: a.reshape(b, s, a.shape[-1])
        y_mla = _attention(r3(q), r3(k), r3(v)).reshape(t, -1)
        y_ret = _retention(r3(rq), r3(rk), r3(rv), r3(rg), g_ret_norm[l]).reshape(t, -1)
        x1, h2, st = _mix(y_mla, y_ret, x2, gt1, sc2, sh2, g_norm2[l].reshape(1, d),
                          w_out[l].astype(BF16), w_query[l].astype(BF16), sub_keys[l].astype(BF16), s)
        eidx4, gate = _topk(st)
        peer = _peer_apply(h2, eidx4, gate, _pack_table(expert_u[l]), _pack_table(expert_v[l]))
        x2 = _residual(x1, peer, gt2, s)
    return _final_norm(x2, g_final).reshape(b, s, d)
```

```python
import functools
import math

import jax
import jax.numpy as jnp
import numpy as np
from jax import lax
from jax.experimental import pallas as pl
from jax.experimental.pallas import tpu as pltpu

F32 = jnp.float32
BF16 = jnp.bfloat16
HIGHEST = lax.Precision.HIGHEST

EPS = 1e-6
ROPE_THETA = 10000.0

MLA_HEADS = 8
MLA_QK_NOPE = 64
MLA_QK_ROPE = 32
MLA_QK_DIM = MLA_QK_NOPE + MLA_QK_ROPE
MLA_V_DIM = 64
MLA_Q_RANK = 256
MLA_KV_RANK = 128

RET_HEADS = 8
RET_HEAD_DIM = 64
RET_WIDTH = RET_HEADS * RET_HEAD_DIM
RET_CHUNK = 128

PEER_HEADS = 8
PEER_N_KEYS = 128
PEER_HALF = 128
PEER_TOPK = 16
PEER_SLOTS = PEER_HEADS * PEER_TOPK

LANES = 128
VMEM_LIMIT = 56 * 1024 * 1024

_NT = (((1,), (1,)), ((), ()))


def _cparams(sem):
    return pltpu.CompilerParams(dimension_semantics=sem, vmem_limit_bytes=VMEM_LIMIT)


def _adaln_kernel(c_ref, w_ref, b_ref, o_ref):
    c = c_ref[...]
    s = c * jax.nn.sigmoid(c)
    o_ref[...] = jnp.dot(s, w_ref[...], precision=HIGHEST, preferred_element_type=F32) + b_ref[...]


def _adaln(c, w, b):
    bsz, d = c.shape
    n = w.shape[1]
    return pl.pallas_call(
        _adaln_kernel,
        grid=(n // d,),
        in_specs=[
            pl.BlockSpec((bsz, d), lambda j: (0, 0)),
            pl.BlockSpec((d, d), lambda j: (0, j)),
            pl.BlockSpec((1, d), lambda j: (0, j)),
        ],
        out_specs=pl.BlockSpec((bsz, d), lambda j: (0, j)),
        out_shape=jax.ShapeDtypeStruct((bsz, n), F32),
        compiler_params=_cparams(("arbitrary",)),
        name="adaln",
    )(c, w, b.reshape(1, n))


def _rope_tab_kernel(pos_ref, inv_ref, sg_ref, cr_ref, sr_ref, cm_ref, sm_ref):
    pos = pos_ref[...]
    ang_r = pos * inv_ref[0:1, :]
    ang_m = pos * inv_ref[1:2, :]
    cr_ref[...] = jnp.cos(ang_r)
    sr_ref[...] = jnp.sin(ang_r) * sg_ref[0:1, :]
    cm_ref[...] = jnp.cos(ang_m)
    sm_ref[...] = jnp.sin(ang_m) * sg_ref[1:2, :]


def _rope_tables(positions):
    t = positions.size
    pos = positions.reshape(t, 1).astype(F32)
    inv_r = ROPE_THETA ** (-jnp.arange(0, RET_HEAD_DIM, 2, dtype=F32) / RET_HEAD_DIM)
    inv_m = ROPE_THETA ** (-jnp.arange(0, MLA_QK_ROPE, 2, dtype=F32) / MLA_QK_ROPE)
    z = lambda n: jnp.zeros((n,), F32)
    o = lambda n: jnp.ones((n,), F32)
    inv = jnp.stack([
        jnp.tile(inv_r, 4),
        jnp.concatenate([inv_m, z(48), inv_m, z(48)]),
    ])
    sg = jnp.stack([
        jnp.concatenate([-o(64), o(64)]),
        jnp.concatenate([-o(16), z(48), o(16), z(48)]),
    ])
    tm = 512
    tab = jax.ShapeDtypeStruct((t, LANES), F32)
    spec = pl.BlockSpec((tm, LANES), lambda i: (i, 0))
    cst = pl.BlockSpec((2, LANES), lambda i: (0, 0))
    return pl.pallas_call(
        _rope_tab_kernel,
        grid=(t // tm,),
        in_specs=[pl.BlockSpec((tm, 1), lambda i: (i, 0)), cst, cst],
        out_specs=[spec] * 4,
        out_shape=[tab] * 4,
        compiler_params=_cparams(("parallel",)),
        name="rope_tables",
    )(pos, inv, sg)


def _rms(x, g):
    return x * lax.rsqrt(jnp.mean(x * x, axis=-1, keepdims=True) + EPS) * g


def _rot(x, c, s):
    return x * c + pltpu.roll(x, 64, 1) * s


def _proj_kernel(x_ref, sc_ref, sh_ref, g1_ref, wa_ref, wr_ref, gq_ref, wuq_ref, gkv_ref,
                 wuk_ref, wuv_ref, cr_ref, sr_ref, cm_ref, sm_ref,
                 q_ref, k_ref, v_ref, rq_ref, rk_ref, rv_ref, rg_ref):
    x = x_ref[...]
    h = _rms(x, g1_ref[...]) * (1.0 + sc_ref[...]) + sh_ref[...]
    hb = h.astype(BF16)
    cm, sm = cm_ref[...], sm_ref[...]
    cr, sr = cr_ref[...], sr_ref[...]

    pa = jnp.dot(hb, wa_ref[...], preferred_element_type=F32)
    q_lat = pa[:, :MLA_Q_RANK]
    kv_lat = pa[:, MLA_Q_RANK:MLA_Q_RANK + MLA_KV_RANK]
    kr = _rot(pa[:, MLA_Q_RANK + MLA_KV_RANK:], cm, sm)

    qn = _rms(q_lat, gq_ref[...]).astype(BF16)
    q = jnp.dot(qn, wuq_ref[...], preferred_element_type=F32)
    scale = MLA_QK_DIM ** -0.5 * math.log2(math.e)
    for hd in range(MLA_HEADS):
        sl = slice(hd * LANES, (hd + 1) * LANES)
        q_ref[:, sl] = (_rot(q[:, sl], cm, sm) * scale).astype(BF16)

    kvn = _rms(kv_lat, gkv_ref[...]).astype(BF16)
    k = jnp.dot(kvn, wuk_ref[...], preferred_element_type=F32)
    for hd in range(MLA_HEADS):
        sl = slice(hd * LANES, (hd + 1) * LANES)
        k_ref[:, sl] = (k[:, sl] + kr).astype(BF16)
    v = jnp.dot(kvn, wuv_ref[...], preferred_element_type=F32)
    vlane = lax.broadcasted_iota(jnp.int32, v.shape, 1) & (LANES - 1)
    v_ref[...] = jnp.where(vlane == MLA_V_DIM, 1.0, v).astype(BF16)

    pr = jnp.dot(hb, wr_ref[...], preferred_element_type=F32)
    w = RET_WIDTH
    for p in range(RET_HEADS // 2):
        sl = slice(p * LANES, (p + 1) * LANES)
        rq_ref[:, sl] = _rot(pr[:, p * LANES:(p + 1) * LANES], cr, sr).astype(BF16)
        rk_ref[:, sl] = (_rot(pr[:, w + p * LANES:w + (p + 1) * LANES], cr, sr)
                         * (RET_HEAD_DIM ** -0.5)).astype(BF16)
    rv_ref[...] = pr[:, 2 * w:3 * w].astype(BF16)
    rg_ref[...] = pr[:, 3 * w:]


def _proj(x2, sc1, sh1, g1, wa, wr, gq, wuq, gkv, wuk, wuv, tabs, seq):
    t, d = x2.shape
    tm = 256
    tpb = seq // tm
    cr, sr, cm, sm = tabs
    row = lambda n: pl.BlockSpec((tm, n), lambda i: (i, 0))
    full = lambda a: pl.BlockSpec(a.shape, lambda i: (0,) * a.ndim)
    mod = pl.BlockSpec((None, 1, d), lambda i: (i // tpb, 0, 0))
    outs = [(8 * LANES, BF16), (8 * LANES, BF16), (8 * LANES, BF16), (512, BF16), (512, BF16),
            (512, BF16), (512, F32)]
    return pl.pallas_call(
        _proj_kernel,
        grid=(t // tm,),
        in_specs=[row(d), mod, mod, full(g1), full(wa), full(wr), full(gq), full(wuq), full(gkv),
                  full(wuk), full(wuv), row(LANES), row(LANES), row(LANES), row(LANES)],
        out_specs=[row(n) for n, _ in outs],
        out_shape=[jax.ShapeDtypeStruct((t, n), dt) for n, dt in outs],
        compiler_params=_cparams(("parallel",)),
        name="in_proj",
    )(x2, sc1, sh1, g1, wa, wr, gq, wuq, gkv, wuk, wuv, cr, sr, cm, sm)


def _attn_kernel(q_ref, k_ref, v_ref, o_ref, *, tq, tk):
    i = pl.program_id(2)
    row = lax.broadcasted_iota(jnp.int32, (tq, tk), 0)
    col = lax.broadcasted_iota(jnp.int32, (tq, tk), 1)
    heads = [slice(hh * LANES, (hh + 1) * LANES) for hh in range(2)]
    qs = [q_ref[:, hs] for hs in heads]

    def step(off, carry, mask):
        new = []
        for hs, q, (m, acc) in zip(heads, qs, carry):
            kb = k_ref[pl.ds(off, tk), hs]
            vb = v_ref[pl.ds(off, tk), hs]
            s = lax.dot_general(q, kb, _NT, preferred_element_type=F32)
            if mask is not None:
                s = jnp.where(mask, s, -1e30)
            m_new = jnp.maximum(m, jnp.max(s, axis=1, keepdims=True))
            p = jnp.exp2(s - m_new)
            acc = jnp.exp2(m - m_new) * acc + jnp.dot(p.astype(BF16), vb, preferred_element_type=F32)
            new.append((m_new, acc))
        return tuple(new)

    carry = ((jnp.full((tq, 1), -1e30, F32), jnp.zeros((tq, LANES), F32)),) * 2
    carry = lax.fori_loop(0, i * (tq // tk), lambda j, c: step(pl.multiple_of(j * tk, tk), c, None), carry)
    for dblk in range(tq // tk):
        carry = step(pl.multiple_of(i * tq + dblk * tk, tk), carry, col + dblk * tk <= row)
    outs = [acc[:, :MLA_V_DIM] / acc[:, MLA_V_DIM:MLA_V_DIM + 1] for _, acc in carry]
    o_ref[...] = jnp.concatenate(outs, axis=1).astype(o_ref.dtype)


def _attention(q, k, v):
    b, s, _ = q.shape
    tq, tk = 1024, 512
    return pl.pallas_call(
        functools.partial(_attn_kernel, tq=tq, tk=tk),
        grid=(b, MLA_HEADS // 2, s // tq),
        in_specs=[
            pl.BlockSpec((None, tq, 2 * LANES), lambda bi, p, i: (bi, i, p)),
            pl.BlockSpec((None, s, 2 * LANES), lambda bi, p, i: (bi, 0, p)),
            pl.BlockSpec((None, s, 2 * LANES), lambda bi, p, i: (bi, 0, p)),
        ],
        out_specs=pl.BlockSpec((None, tq, LANES), lambda bi, p, i: (bi, i, p)),
        out_shape=jax.ShapeDtypeStruct((b, s, MLA_HEADS * MLA_V_DIM), BF16),
        compiler_params=_cparams(("parallel", "parallel", "arbitrary")),
        name="mla_attention",
    )(q, k, v)


def _ret_kernel(q_ref, k_ref, v_ref, g_ref, dm_ref, z_ref, xi_ref, dec_ref, gn_ref, o_ref, st_ref,
                *, nchunk):
    c = RET_CHUNK

    @pl.when(pl.program_id(2) == 0)
    def _():
        st_ref[...] = jnp.zeros_like(st_ref)

    lane = lax.broadcasted_iota(jnp.int32, (c, LANES), 1)
    sub = lax.broadcasted_iota(jnp.int32, (c, LANES), 0)
    v_first = lane < RET_HEAD_DIM
    k_first = (lane & 32) == 0
    same_head = ((sub & 32) == 0) == v_first
    for ci in range(nchunk):
        sl = slice(ci * c, (ci + 1) * c)
        q, k, v = q_ref[sl, :], k_ref[sl, :], v_ref[sl, :]
        zero = jnp.zeros_like(q)
        s_a = lax.dot_general(jnp.where(k_first, q, zero), k, _NT, preferred_element_type=F32) * dm_ref[0]
        s_b = lax.dot_general(jnp.where(k_first, zero, q), k, _NT, preferred_element_type=F32) * dm_ref[1]
        s_ab = jnp.concatenate([s_a, s_b], axis=1).astype(BF16)
        v_bd = jnp.concatenate([jnp.where(v_first, v, zero), jnp.where(v_first, zero, v)], axis=0)
        y = jnp.dot(s_ab, v_bd, preferred_element_type=F32)

        st = st_ref[...]
        st_hi = st.astype(BF16)
        st_lo = (st - st_hi.astype(F32)).astype(BF16)
        y = y + (jnp.dot(q, st_hi, preferred_element_type=F32)
                 + jnp.dot(q, st_lo, preferred_element_type=F32)) * xi_ref[...]

        vz = (v.astype(F32) * z_ref[...]).astype(BF16)
        kt = k.astype(F32).T.astype(BF16)
        kv = jnp.dot(kt, vz, preferred_element_type=F32)
        st_ref[...] = st * dec_ref[...] + jnp.where(same_head, kv, 0.0)

        def head_mean(a):
            tot = jnp.sum(a, axis=1, keepdims=True)
            first = jnp.sum(jnp.where(v_first, a, 0.0), axis=1, keepdims=True)
            return jnp.where(v_first, first, tot - first) * (1.0 / RET_HEAD_DIM)

        dlt = y - head_mean(y)
        yn = dlt * lax.rsqrt(head_mean(dlt * dlt) + EPS) * gn_ref[...]
        gate = g_ref[sl, :]
        o_ref[sl, :] = (gate * jax.nn.sigmoid(gate) * yn).astype(o_ref.dtype)


def _retention_consts():
    h, c = RET_HEADS, RET_CHUNK
    gamma = 1.0 - 2.0 ** (-5.0 - jnp.arange(h, dtype=F32))
    log_g = jnp.log(gamma)
    idx = jnp.arange(c, dtype=F32)
    diff = idx[:, None] - idx[None, :]
    dmask = jnp.where(diff >= 0, jnp.exp(log_g[:, None, None] * jnp.maximum(diff, 0.0)), 0.0)
    zeta = jnp.exp(log_g[:, None] * (c - 1.0 - idx))
    xi = jnp.exp(log_g[:, None] * (idx + 1.0))
    decay = jnp.exp(log_g * c)
    by_lane = lambda a: jnp.repeat(a.reshape(h // 2, 2, -1), RET_HEAD_DIM, axis=1)
    z = by_lane(zeta).transpose(0, 2, 1)
    x = by_lane(xi).transpose(0, 2, 1)
    dec = by_lane(decay[:, None]).transpose(0, 2, 1)
    return dmask, z, x, dec


def _retention(rq, rk, rv, rg, g_ret):
    b, s, w = rq.shape
    tc = 512
    dmask, z, xi, dec = _retention_consts()
    blk = pl.BlockSpec((None, tc, LANES), lambda bi, p, t: (bi, t, p))
    per_pair = lambda shp: pl.BlockSpec((None,) + shp, lambda bi, p, t: (p, 0, 0))
    return pl.pallas_call(
        functools.partial(_ret_kernel, nchunk=tc // RET_CHUNK),
        grid=(b, RET_HEADS // 2, s // tc),
        in_specs=[blk, blk, blk, blk,
                  pl.BlockSpec((2, RET_CHUNK, RET_CHUNK), lambda bi, p, t: (p, 0, 0)),
                  per_pair((RET_CHUNK, LANES)), per_pair((RET_CHUNK, LANES)), per_pair((1, LANES)),
                  pl.BlockSpec((1, LANES), lambda bi, p, t: (0, p))],
        out_specs=blk,
        out_shape=jax.ShapeDtypeStruct((b, s, w), BF16),
        scratch_shapes=[pltpu.VMEM((LANES, LANES), F32)],
        compiler_params=_cparams(("parallel", "parallel", "arbitrary")),
        name="retention",
    )(rq, rk, rv, rg, dmask, z, xi, dec, g_ret.reshape(1, w))


def _mix_kernel(ym_ref, yr_ref, x_ref, gt_ref, sc_ref, sh_ref, g2_ref, wo_ref, wq_ref, keys_ref,
                x1_ref, h2_ref, st_ref):
    half = ym_ref.shape[1]
    mixed = (jnp.dot(ym_ref[...], wo_ref[:half, :], preferred_element_type=F32)
             + jnp.dot(yr_ref[...], wo_ref[half:, :], preferred_element_type=F32))
    x1 = x_ref[...] + gt_ref[...] * mixed
    x1_ref[...] = x1
    h2 = _rms(x1, g2_ref[...]) * (1.0 + sc_ref[...]) + sh_ref[...]
    h2_ref[...] = h2
    pq = jnp.dot(h2.astype(BF16), wq_ref[...], preferred_element_type=F32)
    for g in range(2 * PEER_HEADS):
        qg = pq[:, g * PEER_HALF:(g + 1) * PEER_HALF].astype(BF16)
        st_ref[g * PEER_N_KEYS:(g + 1) * PEER_N_KEYS, :] = lax.dot_general(
            keys_ref[g % 2], qg, _NT, preferred_element_type=F32)


def _mix(ym, yr, x2, gt1, sc2, sh2, g2, wo, wq, keys, seq):
    t, d = x2.shape
    tm = 256
    tpb = seq // tm
    row = lambda n: pl.BlockSpec((tm, n), lambda i: (i, 0))
    full = lambda a: pl.BlockSpec(a.shape, lambda i: (0,) * a.ndim)
    mod = pl.BlockSpec((None, 1, d), lambda i: (i // tpb, 0, 0))
    ns = 2 * PEER_HEADS * PEER_N_KEYS
    return pl.pallas_call(
        _mix_kernel,
        grid=(t // tm,),
        in_specs=[row(ym.shape[1]), row(yr.shape[1]), row(d), mod, mod, mod, full(g2), full(wo),
                  full(wq), full(keys)],
        out_specs=[row(d), row(d), pl.BlockSpec((ns, tm), lambda i: (0, i))],
        out_shape=[jax.ShapeDtypeStruct((t, d), F32), jax.ShapeDtypeStruct((t, d), F32),
                   jax.ShapeDtypeStruct((ns, t), F32)],
        compiler_params=_cparams(("parallel",)),
        name="out_proj_peer_scores",
    )(ym, yr, x2, gt1, sc2, sh2, g2, wo, wq, keys)


def _top16(s, payload=None):
    rows = lax.broadcasted_iota(jnp.int32, s.shape, 0).astype(F32)
    vals, sel = [], []
    for _ in range(PEER_TOPK):
        m = jnp.max(s, axis=0, keepdims=True)
        at = jnp.min(jnp.where(s == m, rows, float(s.shape[0])), axis=0, keepdims=True)
        hit = rows == at
        vals.append(m)
        sel.append(at if payload is None else jnp.max(jnp.where(hit, payload, -1.0), axis=0, keepdims=True))
        s = jnp.where(hit, -jnp.inf, s)
    return jnp.concatenate(vals, axis=0), jnp.concatenate(sel, axis=0)


def _pair_grid(r0, r1, combine, fill):
    k = PEER_TOPK
    sub = lax.broadcasted_iota(jnp.int32, (8, r0.shape[1]), 0)
    parts = [combine(r0[0:1], r1), combine(r0[1:2], r1[0:8])]
    for a in range(2, 8):
        parts.append(jnp.where(sub < k // (a + 1), combine(r0[a:a + 1], r1[0:8]), fill))
    parts.append(combine(r0[8:16], r1[0:1]))
    return jnp.concatenate(parts, axis=0)


def _topk_kernel(st_ref, e_ref, g_ref, es_ref, gs_ref):
    nk, k = PEER_N_KEYS, PEER_TOPK

    def head(h, _):
        base = pl.multiple_of(h * 2 * nk, 2 * nk)
        v0, i0 = _top16(st_ref[pl.ds(base, nk), :])
        v1, i1 = _top16(st_ref[pl.ds(base + nk, nk), :])
        cand = _pair_grid(v0, v1, lambda x, y: x + y, -jnp.inf)
        cidx = _pair_grid(i0, i1, lambda x, y: x * float(nk) + y, 0.0)
        best, eidx = _top16(cand, cidx)
        ex = jnp.exp(best - jnp.max(best, axis=0, keepdims=True))
        gate = ex / jnp.sum(ex, axis=0, keepdims=True)
        row = pl.multiple_of(h * k, k)
        es_ref[pl.ds(row, k), :] = eidx.astype(jnp.int32)
        gs_ref[pl.ds(row, k), :] = gate
        return 0

    lax.fori_loop(0, PEER_HEADS, head, 0)
    e_ref[...] = es_ref[...].T * ROWS_PER_EXPERT
    g_ref[...] = gs_ref[...].T


def _topk(st):
    ns, t = st.shape
    tt = 256
    out = pl.BlockSpec((tt, PEER_SLOTS), lambda i: (i, 0))
    return pl.pallas_call(
        _topk_kernel,
        grid=(t // tt,),
        in_specs=[pl.BlockSpec((ns, tt), lambda i: (0, i))],
        out_specs=[out, out],
        out_shape=[jax.ShapeDtypeStruct((t, PEER_SLOTS), jnp.int32),
                   jax.ShapeDtypeStruct((t, PEER_SLOTS), F32)],
        scratch_shapes=[pltpu.VMEM((PEER_SLOTS, tt), jnp.int32), pltpu.VMEM((PEER_SLOTS, tt), F32)],
        compiler_params=_cparams(("parallel",)),
        name="peer_topk",
    )(st)


PEER_TB = 128
ROWS_PER_EXPERT = 4
TILE_ROWS = PEER_SLOTS * ROWS_PER_EXPERT


def _pack_table(tab):
    e, d = tab.shape
    tb = lax.bitcast_convert_type(tab.astype(BF16), jnp.uint16).astype(jnp.uint32)
    tb = tb.reshape(e, 2, ROWS_PER_EXPERT, LANES)
    word = tb[:, 0] | (tb[:, 1] << 16)
    return lax.bitcast_convert_type(word, jnp.int32).reshape(e * ROWS_PER_EXPERT, LANES)


def _peer_layout():
    j = np.arange(2 * TILE_ROWS)
    chunk = (j % 8) // 2 + 4 * (j % 2)
    mask8 = (chunk[None, :] == np.arange(8)[:, None]).astype(np.float32)
    group = (j[:, None] // 8 == np.arange(PEER_SLOTS)[None, :]).astype(np.float32)
    return jnp.asarray(mask8), jnp.asarray(group), jnp.asarray(group.T)


def _load_table(tab_hbm, tab_vmem, sem):
    @pl.when(pl.program_id(0) == 0)
    def _():
        cp = pltpu.make_async_copy(tab_hbm, tab_vmem, sem)
        cp.start()
        cp.wait()


def _gather_tile(idx_ref, tab_ref, tile_ref, token):
    tok_idx = idx_ref.at[pl.ds(token * PEER_SLOTS, PEER_SLOTS)]
    for s in range(PEER_SLOTS):
        row = pl.multiple_of(tok_idx[s], ROWS_PER_EXPERT)
        tile_ref[pl.ds(s * ROWS_PER_EXPERT, ROWS_PER_EXPERT), :] = tab_ref[pl.ds(row, ROWS_PER_EXPERT), :]
    return pltpu.bitcast(tile_ref[...], BF16)


def _split_bf16(a):
    hi = a.astype(BF16)
    lo = (a - hi.astype(F32)).astype(BF16)
    return jnp.concatenate([hi, lo], axis=0)


def _peer_u_kernel(idx_ref, x_ref, mask_ref, tab_hbm, o_ref, tab_ref, tile_a, tile_b, sem):
    _load_table(tab_hbm, tab_ref, sem)
    mask8 = mask_ref[...]

    def group(t8, _):
        rows = []
        for j in range(8):
            tok = t8 * 8 + j
            tile = _gather_tile(idx_ref, tab_ref, tile_a if j % 2 == 0 else tile_b, tok)
            d = lax.dot_general(_split_bf16(x_ref[tok]), tile, _NT, preferred_element_type=F32)
            rows.append(jnp.sum((d[:8] + d[8:]) * mask8, axis=0, keepdims=True))
        o_ref[pl.ds(pl.multiple_of(t8 * 8, 8), 8), :] = jnp.concatenate(rows, axis=0)
        return 0

    lax.fori_loop(0, PEER_TB // 8, group, 0)


def _peer_v_kernel(idx_ref, a_ref, g_ref, mask_ref, grp_ref, grpt_ref, tab_hbm, o_ref,
                   tab_ref, tile_a, tile_b, wx_ref, sem):
    _load_table(tab_hbm, tab_ref, sem)
    mask8 = mask_ref[...]
    act = jnp.dot(a_ref[...], grp_ref[...], precision=HIGHEST, preferred_element_type=F32)
    gelu = 0.5 * act * (1.0 + lax.erf(act * math.sqrt(0.5)))
    w = g_ref[...] * gelu
    wx_ref[...] = jnp.dot(w, grpt_ref[...], precision=HIGHEST, preferred_element_type=F32)

    def group(t8, _):
        w8 = wx_ref[pl.ds(pl.multiple_of(t8 * 8, 8), 8), :]
        for j in range(8):
            tok = t8 * 8 + j
            tile = _gather_tile(idx_ref, tab_ref, tile_a if j % 2 == 0 else tile_b, tok)
            lhs = _split_bf16(w8[j:j + 1, :] * mask8)
            out = jnp.dot(lhs, tile, preferred_element_type=F32)
            o_ref[tok] = out[:8] + out[8:]
        return 0

    lax.fori_loop(0, PEER_TB // 8, group, 0)


def _peer_scratch():
    n_rows = PEER_N_KEYS * PEER_N_KEYS * ROWS_PER_EXPERT
    return [pltpu.VMEM((n_rows, LANES), jnp.int32), pltpu.VMEM((TILE_ROWS, LANES), jnp.int32),
            pltpu.VMEM((TILE_ROWS, LANES), jnp.int32)]


def _peer_apply(h2, eidx4, gate, tab_u, tab_v):
    t, d = h2.shape
    tb = PEER_TB
    mask8, grp, grpt = _peer_layout()
    idx = eidx4.reshape(t * PEER_SLOTS)
    idx_spec = pl.BlockSpec((tb * PEER_SLOTS,), lambda i: (i,), memory_space=pltpu.SMEM)
    full = lambda a: pl.BlockSpec(a.shape, lambda i: (0,) * a.ndim)
    tok3 = pl.BlockSpec((tb, 8, LANES), lambda i: (i, 0, 0))
    hbm = pl.BlockSpec(memory_space=pl.ANY)
    act = pl.pallas_call(
        _peer_u_kernel,
        grid=(t // tb,),
        in_specs=[idx_spec, tok3, full(mask8), hbm],
        out_specs=pl.BlockSpec((tb, 2 * TILE_ROWS), lambda i: (i, 0)),
        out_shape=jax.ShapeDtypeStruct((t, 2 * TILE_ROWS), F32),
        scratch_shapes=_peer_scratch() + [pltpu.SemaphoreType.DMA(())],
        compiler_params=_cparams(("arbitrary",)),
        name="peer_u",
    )(idx, h2.reshape(t, 8, LANES), mask8, tab_u)
    out = pl.pallas_call(
        _peer_v_kernel,
        grid=(t // tb,),
        in_specs=[idx_spec, pl.BlockSpec((tb, 2 * TILE_ROWS), lambda i: (i, 0)),
                  pl.BlockSpec((tb, PEER_SLOTS), lambda i: (i, 0)), full(mask8), full(grp), full(grpt), hbm],
        out_specs=tok3,
        out_shape=jax.ShapeDtypeStruct((t, 8, LANES), F32),
        scratch_shapes=_peer_scratch() + [pltpu.VMEM((tb, 2 * TILE_ROWS), F32), pltpu.SemaphoreType.DMA(())],
        compiler_params=_cparams(("arbitrary",)),
        name="peer_v",
    )(idx, act, gate, mask8, grp, grpt, tab_v)
    return out.reshape(t, d)


def _resid_kernel(x_ref, p_ref, gt_ref, g_ref, o_ref, *, final):
    x = x_ref[...] + gt_ref[...] * p_ref[...]
    o_ref[...] = _rms(x, g_ref[...]) if final else x


def _residual(x1, peer, gt2, g_final, seq, final):
    t, d = x1.shape
    tm = 512
    tpb = seq // tm
    row = pl.BlockSpec((tm, d), lambda i: (i, 0))
    return pl.pallas_call(
        functools.partial(_resid_kernel, final=final),
        grid=(t // tm,),
        in_specs=[row, row, pl.BlockSpec((None, 1, d), lambda i: (i // tpb, 0, 0)),
                  pl.BlockSpec((1, d), lambda i: (0, 0))],
        out_specs=row,
        out_shape=jax.ShapeDtypeStruct((t, d), F32),
        compiler_params=_cparams(("parallel",)),
        name="peer_residual",
    )(x1, peer, gt2, g_final.reshape(1, d))


def _mla_head_cols(rope_cols, nope_cols):
    pad = lambda n: [-1] * n
    r1 = list(rope_cols[:16]) if rope_cols is not None else pad(16)
    r2 = list(rope_cols[16:]) if rope_cols is not None else pad(16)
    n1 = list(nope_cols[:48]) if nope_cols is not None else pad(48)
    n2 = list(nope_cols[48:]) if nope_cols is not None else pad(16)
    return r1 + n1 + r2 + n2 + pad(32)


def _take_cols(w, cols):
    cols = np.asarray(cols)
    out = jnp.take(w, jnp.asarray(np.maximum(cols, 0)), axis=1)
    return jnp.where(jnp.asarray(cols >= 0)[None, :], out, 0.0)


def _layer_weights(w_in, w_uq, w_ukv):
    qk = MLA_QK_DIM
    uq_cols, uk_cols, uv_cols = [], [], []
    for h in range(MLA_HEADS):
        uq_cols += _mla_head_cols(range(h * qk + MLA_QK_NOPE, (h + 1) * qk), range(h * qk, h * qk + MLA_QK_NOPE))
        kv0 = h * (MLA_QK_NOPE + MLA_V_DIM)
        uk_cols += _mla_head_cols(None, range(kv0, kv0 + MLA_QK_NOPE))
        uv_cols += list(range(kv0 + MLA_QK_NOPE, kv0 + MLA_QK_NOPE + MLA_V_DIM)) + [-1] * (LANES - MLA_V_DIM)
    o_kr = MLA_Q_RANK + MLA_KV_RANK
    o_r = o_kr + MLA_QK_ROPE
    a_cols = list(range(o_kr)) + _mla_head_cols(range(o_kr, o_r), None)

    def pair_cols(base):
        cols = []
        for p in range(RET_HEADS // 2):
            a, b = base + 2 * p * RET_HEAD_DIM, base + (2 * p + 1) * RET_HEAD_DIM
            cols += list(range(a, a + 32)) + list(range(b, b + 32)) + list(range(a + 32, a + 64)) + list(range(b + 32, b + 64))
        return cols

    r_cols = (pair_cols(o_r) + pair_cols(o_r + RET_WIDTH)
              + list(range(o_r + 2 * RET_WIDTH, o_r + 4 * RET_WIDTH)))
    bf = lambda a: a.astype(BF16)
    return (bf(_take_cols(w_in, a_cols)), bf(_take_cols(w_in, r_cols)), bf(_take_cols(w_uq, uq_cols)),
            bf(_take_cols(w_ukv, uk_cols)), bf(_take_cols(w_ukv, uv_cols)))


def kernel(x, c, positions, w_ada, b_ada, g_norm1, w_in, g_q_norm, w_uq, g_kv_norm, w_ukv, g_ret_norm,
           w_out, g_norm2, w_query, sub_keys, expert_u, expert_v, g_final):
    b, s, d = x.shape
    t = b * s
    depth = w_ada.shape[0]
    tabs = _rope_tables(positions)
    x2 = x.reshape(t, d)
    for l in range(depth):
        mod = _adaln(c, w_ada[l], b_ada[l])
        sh1, sc1, gt1, sh2, sc2, gt2 = [m.reshape(b, 1, d) for m in jnp.split(mod, 6, axis=-1)]
        wa, wr, wuq, wuk, wuv = _layer_weights(w_in[l], w_uq[l], w_ukv[l])
        q, k, v, rq, rk, rv, rg = _proj(
            x2, sc1, sh1, g_norm1[l].reshape(1, d), wa, wr, g_q_norm[l].reshape(1, -1), wuq,
            g_kv_norm[l].reshape(1, -1), wuk, wuv, tabs, s)
        r3 = lambda a: a.reshape(b, s, a.shape[-1])
        y_mla = _attention(r3(q), r3(k), r3(v)).reshape(t, -1)
        y_ret = _retention(r3(rq), r3(rk), r3(rv), r3(rg), g_ret_norm[l]).reshape(t, -1)
        x1, h2, st = _mix(y_mla, y_ret, x2, gt1, sc2, sh2, g_norm2[l].reshape(1, d),
                          w_out[l].astype(BF16), w_query[l].astype(BF16), sub_keys[l].astype(BF16), s)
        eidx4, gate = _topk(st)
        peer = _peer_apply(h2, eidx4, gate, _pack_table(expert_u[l]), _pack_table(expert_v[l]))
        x2 = _residual(x1, peer, gt2, g_final, s, final=(l == depth - 1))
    return x2.reshape(b, s, d)
```

```python
import functools
import math

import jax
import jax.numpy as jnp
import numpy as np
from jax import lax
from jax.experimental import pallas as pl
from jax.experimental.pallas import tpu as pltpu

F32 = jnp.float32
BF16 = jnp.bfloat16
HIGHEST = lax.Precision.HIGHEST

EPS = 1e-6
ROPE_THETA = 10000.0

MLA_HEADS = 8
MLA_QK_NOPE = 64
MLA_QK_ROPE = 32
MLA_QK_DIM = MLA_QK_NOPE + MLA_QK_ROPE
MLA_V_DIM = 64
MLA_Q_RANK = 256
MLA_KV_RANK = 128

RET_HEADS = 8
RET_HEAD_DIM = 64
RET_WIDTH = RET_HEADS * RET_HEAD_DIM
RET_CHUNK = 128

PEER_HEADS = 8
PEER_N_KEYS = 128
PEER_HALF = 128
PEER_TOPK = 16
PEER_SLOTS = PEER_HEADS * PEER_TOPK

LANES = 128
VMEM_LIMIT = 56 * 1024 * 1024

_NT = (((1,), (1,)), ((), ()))


def _cparams(sem):
    return pltpu.CompilerParams(dimension_semantics=sem, vmem_limit_bytes=VMEM_LIMIT)


def _adaln_kernel(c_ref, w_ref, b_ref, o_ref):
    c = c_ref[...]
    s = c * jax.nn.sigmoid(c)
    o_ref[...] = jnp.dot(s, w_ref[...], precision=HIGHEST, preferred_element_type=F32) + b_ref[...]


def _adaln(c, w, b):
    bsz, d = c.shape
    n = w.shape[1]
    return pl.pallas_call(
        _adaln_kernel,
        grid=(n // d,),
        in_specs=[
            pl.BlockSpec((bsz, d), lambda j: (0, 0)),
            pl.BlockSpec((d, d), lambda j: (0, j)),
            pl.BlockSpec((1, d), lambda j: (0, j)),
        ],
        out_specs=pl.BlockSpec((bsz, d), lambda j: (0, j)),
        out_shape=jax.ShapeDtypeStruct((bsz, n), F32),
        compiler_params=_cparams(("arbitrary",)),
        name="adaln",
    )(c, w, b.reshape(1, n))


def _rope_tab_kernel(pos_ref, inv_ref, sg_ref, cr_ref, sr_ref, cm_ref, sm_ref):
    pos = pos_ref[...]
    ang_r = pos * inv_ref[0:1, :]
    ang_m = pos * inv_ref[1:2, :]
    cr_ref[...] = jnp.cos(ang_r)
    sr_ref[...] = jnp.sin(ang_r) * sg_ref[0:1, :]
    cm_ref[...] = jnp.cos(ang_m)
    sm_ref[...] = jnp.sin(ang_m) * sg_ref[1:2, :]


def _rope_tables(positions):
    t = positions.size
    pos = positions.reshape(t, 1).astype(F32)
    inv_r = ROPE_THETA ** (-jnp.arange(0, RET_HEAD_DIM, 2, dtype=F32) / RET_HEAD_DIM)
    inv_m = ROPE_THETA ** (-jnp.arange(0, MLA_QK_ROPE, 2, dtype=F32) / MLA_QK_ROPE)
    z = lambda n: jnp.zeros((n,), F32)
    o = lambda n: jnp.ones((n,), F32)
    inv = jnp.stack([
        jnp.tile(inv_r, 4),
        jnp.concatenate([inv_m, z(48), inv_m, z(48)]),
    ])
    sg = jnp.stack([
        jnp.concatenate([-o(64), o(64)]),
        jnp.concatenate([-o(16), z(48), o(16), z(48)]),
    ])
    tm = 512
    tab = jax.ShapeDtypeStruct((t, LANES), F32)
    spec = pl.BlockSpec((tm, LANES), lambda i: (i, 0))
    cst = pl.BlockSpec((2, LANES), lambda i: (0, 0))
    return pl.pallas_call(
        _rope_tab_kernel,
        grid=(t // tm,),
        in_specs=[pl.BlockSpec((tm, 1), lambda i: (i, 0)), cst, cst],
        out_specs=[spec] * 4,
        out_shape=[tab] * 4,
        compiler_params=_cparams(("parallel",)),
        name="rope_tables",
    )(pos, inv, sg)


def _rms(x, g):
    return x * lax.rsqrt(jnp.mean(x * x, axis=-1, keepdims=True) + EPS) * g


def _rot(x, c, s):
    return x * c + pltpu.roll(x, 64, 1) * s


def _proj_kernel(x_ref, sc_ref, sh_ref, g1_ref, wa_ref, wr_ref, gq_ref, wuq_ref, gkv_ref,
                 wuk_ref, wuv_ref, cr_ref, sr_ref, cm_ref, sm_ref,
                 q_ref, k_ref, v_ref, rq_ref, rk_ref, rv_ref, rg_ref):
    x = x_ref[...]
    h = _rms(x, g1_ref[...]) * (1.0 + sc_ref[...]) + sh_ref[...]
    hb = h.astype(BF16)
    cm, sm = cm_ref[...], sm_ref[...]
    cr, sr = cr_ref[...], sr_ref[...]

    pa = jnp.dot(hb, wa_ref[...], preferred_element_type=F32)
    q_lat = pa[:, :MLA_Q_RANK]
    kv_lat = pa[:, MLA_Q_RANK:MLA_Q_RANK + MLA_KV_RANK]
    kr = _rot(pa[:, MLA_Q_RANK + MLA_KV_RANK:], cm, sm)

    qn = _rms(q_lat, gq_ref[...]).astype(BF16)
    q = jnp.dot(qn, wuq_ref[...], preferred_element_type=F32)
    scale = MLA_QK_DIM ** -0.5 * math.log2(math.e)
    for hd in range(MLA_HEADS):
        sl = slice(hd * LANES, (hd + 1) * LANES)
        q_ref[:, sl] = (_rot(q[:, sl], cm, sm) * scale).astype(BF16)

    kvn = _rms(kv_lat, gkv_ref[...]).astype(BF16)
    k = jnp.dot(kvn, wuk_ref[...], preferred_element_type=F32)
    for hd in range(MLA_HEADS):
        sl = slice(hd * LANES, (hd + 1) * LANES)
        k_ref[:, sl] = (k[:, sl] + kr).astype(BF16)
    v = jnp.dot(kvn, wuv_ref[...], preferred_element_type=F32)
    vlane = lax.broadcasted_iota(jnp.int32, v.shape, 1) & (LANES - 1)
    v_ref[...] = jnp.where(vlane == MLA_V_DIM, 1.0, v).astype(BF16)

    pr = jnp.dot(hb, wr_ref[...], preferred_element_type=F32)
    w = RET_WIDTH
    for p in range(RET_HEADS // 2):
        sl = slice(p * LANES, (p + 1) * LANES)
        rq_ref[:, sl] = _rot(pr[:, p * LANES:(p + 1) * LANES], cr, sr).astype(BF16)
        rk_ref[:, sl] = (_rot(pr[:, w + p * LANES:w + (p + 1) * LANES], cr, sr)
                         * (RET_HEAD_DIM ** -0.5)).astype(BF16)
    rv_ref[...] = pr[:, 2 * w:3 * w].astype(BF16)
    rg_ref[...] = pr[:, 3 * w:]


def _proj(x2, sc1, sh1, g1, wa, wr, gq, wuq, gkv, wuk, wuv, tabs, seq):
    t, d = x2.shape
    tm = 256
    tpb = seq // tm
    cr, sr, cm, sm = tabs
    row = lambda n: pl.BlockSpec((tm, n), lambda i: (i, 0))
    full = lambda a: pl.BlockSpec(a.shape, lambda i: (0,) * a.ndim)
    mod = pl.BlockSpec((None, 1, d), lambda i: (i // tpb, 0, 0))
    outs = [(8 * LANES, BF16), (8 * LANES, BF16), (8 * LANES, BF16), (512, BF16), (512, BF16),
            (512, BF16), (512, F32)]
    return pl.pallas_call(
        _proj_kernel,
        grid=(t // tm,),
        in_specs=[row(d), mod, mod, full(g1), full(wa), full(wr), full(gq), full(wuq), full(gkv),
                  full(wuk), full(wuv), row(LANES), row(LANES), row(LANES), row(LANES)],
        out_specs=[row(n) for n, _ in outs],
        out_shape=[jax.ShapeDtypeStruct((t, n), dt) for n, dt in outs],
        compiler_params=_cparams(("parallel",)),
        name="in_proj",
    )(x2, sc1, sh1, g1, wa, wr, gq, wuq, gkv, wuk, wuv, cr, sr, cm, sm)


def _attn_kernel(q_ref, k_ref, v_ref, o_ref, *, tq, tk):
    i = pl.program_id(2)
    row = lax.broadcasted_iota(jnp.int32, (tq, tk), 0)
    col = lax.broadcasted_iota(jnp.int32, (tq, tk), 1)
    heads = [slice(hh * LANES, (hh + 1) * LANES) for hh in range(2)]
    qs = [q_ref[:, hs] for hs in heads]

    def step(off, carry, mask):
        new = []
        for hs, q, (m, acc) in zip(heads, qs, carry):
            kb = k_ref[pl.ds(off, tk), hs]
            vb = v_ref[pl.ds(off, tk), hs]
            s = lax.dot_general(q, kb, _NT, preferred_element_type=F32)
            if mask is not None:
                s = jnp.where(mask, s, -1e30)
            m_new = jnp.maximum(m, jnp.max(s, axis=1, keepdims=True))
            p = jnp.exp2(s - m_new)
            acc = jnp.exp2(m - m_new) * acc + jnp.dot(p.astype(BF16), vb, preferred_element_type=F32)
            new.append((m_new, acc))
        return tuple(new)

    carry = ((jnp.full((tq, 1), -1e30, F32), jnp.zeros((tq, LANES), F32)),) * 2
    carry = lax.fori_loop(0, i * (tq // tk), lambda j, c: step(pl.multiple_of(j * tk, tk), c, None), carry)
    for dblk in range(tq // tk):
        carry = step(pl.multiple_of(i * tq + dblk * tk, tk), carry, col + dblk * tk <= row)
    outs = [acc[:, :MLA_V_DIM] / acc[:, MLA_V_DIM:MLA_V_DIM + 1] for _, acc in carry]
    o_ref[...] = jnp.concatenate(outs, axis=1).astype(o_ref.dtype)


def _attention(q, k, v):
    b, s, _ = q.shape
    tq, tk = 1024, 1024
    return pl.pallas_call(
        functools.partial(_attn_kernel, tq=tq, tk=tk),
        grid=(b, MLA_HEADS // 2, s // tq),
        in_specs=[
            pl.BlockSpec((None, tq, 2 * LANES), lambda bi, p, i: (bi, i, p)),
            pl.BlockSpec((None, s, 2 * LANES), lambda bi, p, i: (bi, 0, p)),
            pl.BlockSpec((None, s, 2 * LANES), lambda bi, p, i: (bi, 0, p)),
        ],
        out_specs=pl.BlockSpec((None, tq, LANES), lambda bi, p, i: (bi, i, p)),
        out_shape=jax.ShapeDtypeStruct((b, s, MLA_HEADS * MLA_V_DIM), BF16),
        compiler_params=_cparams(("parallel", "parallel", "arbitrary")),
        name="mla_attention",
    )(q, k, v)


def _ret_kernel(q_ref, k_ref, v_ref, g_ref, dm_ref, z_ref, xi_ref, dec_ref, gn_ref, o_ref, st_ref,
                *, nchunk):
    c = RET_CHUNK

    @pl.when(pl.program_id(2) == 0)
    def _():
        st_ref[...] = jnp.zeros_like(st_ref)

    lane = lax.broadcasted_iota(jnp.int32, (c, LANES), 1)
    sub = lax.broadcasted_iota(jnp.int32, (c, LANES), 0)
    v_first = lane < RET_HEAD_DIM
    k_first = (lane & 32) == 0
    same_head = ((sub & 32) == 0) == v_first
    for ci in range(nchunk):
        sl = slice(ci * c, (ci + 1) * c)
        q, k, v = q_ref[sl, :], k_ref[sl, :], v_ref[sl, :]
        zero = jnp.zeros_like(q)
        s_a = lax.dot_general(jnp.where(k_first, q, zero), k, _NT, preferred_element_type=F32) * dm_ref[0]
        s_b = lax.dot_general(jnp.where(k_first, zero, q), k, _NT, preferred_element_type=F32) * dm_ref[1]
        s_ab = jnp.concatenate([s_a, s_b], axis=1).astype(BF16)
        v_bd = jnp.concatenate([jnp.where(v_first, v, zero), jnp.where(v_first, zero, v)], axis=0)
        y = jnp.dot(s_ab, v_bd, preferred_element_type=F32)

        st = st_ref[...]
        st_hi = st.astype(BF16)
        st_lo = (st - st_hi.astype(F32)).astype(BF16)
        y = y + (jnp.dot(q, st_hi, preferred_element_type=F32)
                 + jnp.dot(q, st_lo, preferred_element_type=F32)) * xi_ref[...]

        vz = (v.astype(F32) * z_ref[...]).astype(BF16)
        kt = k.astype(F32).T.astype(BF16)
        kv = jnp.dot(kt, vz, preferred_element_type=F32)
        st_ref[...] = st * dec_ref[...] + jnp.where(same_head, kv, 0.0)

        def head_mean(a):
            tot = jnp.sum(a, axis=1, keepdims=True)
            first = jnp.sum(jnp.where(v_first, a, 0.0), axis=1, keepdims=True)
            return jnp.where(v_first, first, tot - first) * (1.0 / RET_HEAD_DIM)

        dlt = y - head_mean(y)
        yn = dlt * lax.rsqrt(head_mean(dlt * dlt) + EPS) * gn_ref[...]
        gate = g_ref[sl, :]
        o_ref[sl, :] = (gate * jax.nn.sigmoid(gate) * yn).astype(o_ref.dtype)


def _retention_consts():
    h, c = RET_HEADS, RET_CHUNK
    gamma = 1.0 - 2.0 ** (-5.0 - jnp.arange(h, dtype=F32))
    log_g = jnp.log(gamma)
    idx = jnp.arange(c, dtype=F32)
    diff = idx[:, None] - idx[None, :]
    dmask = jnp.where(diff >= 0, jnp.exp(log_g[:, None, None] * jnp.maximum(diff, 0.0)), 0.0)
    zeta = jnp.exp(log_g[:, None] * (c - 1.0 - idx))
    xi = jnp.exp(log_g[:, None] * (idx + 1.0))
    decay = jnp.exp(log_g * c)
    by_lane = lambda a: jnp.repeat(a.reshape(h // 2, 2, -1), RET_HEAD_DIM, axis=1)
    z = by_lane(zeta).transpose(0, 2, 1)
    x = by_lane(xi).transpose(0, 2, 1)
    dec = by_lane(decay[:, None]).transpose(0, 2, 1)
    return dmask, z, x, dec


def _retention(rq, rk, rv, rg, g_ret):
    b, s, w = rq.shape
    tc = 512
    dmask, z, xi, dec = _retention_consts()
    blk = pl.BlockSpec((None, tc, LANES), lambda bi, p, t: (bi, t, p))
    per_pair = lambda shp: pl.BlockSpec((None,) + shp, lambda bi, p, t: (p, 0, 0))
    return pl.pallas_call(
        functools.partial(_ret_kernel, nchunk=tc // RET_CHUNK),
        grid=(b, RET_HEADS // 2, s // tc),
        in_specs=[blk, blk, blk, blk,
                  pl.BlockSpec((2, RET_CHUNK, RET_CHUNK), lambda bi, p, t: (p, 0, 0)),
                  per_pair((RET_CHUNK, LANES)), per_pair((RET_CHUNK, LANES)), per_pair((1, LANES)),
                  pl.BlockSpec((1, LANES), lambda bi, p, t: (0, p))],
        out_specs=blk,
        out_shape=jax.ShapeDtypeStruct((b, s, w), BF16),
        scratch_shapes=[pltpu.VMEM((LANES, LANES), F32)],
        compiler_params=_cparams(("parallel", "parallel", "arbitrary")),
        name="retention",
    )(rq, rk, rv, rg, dmask, z, xi, dec, g_ret.reshape(1, w))


def _mix_kernel(ym_ref, yr_ref, x_ref, gt_ref, sc_ref, sh_ref, g2_ref, wo_ref, wq_ref, keys_ref,
                x1_ref, h2_ref, st_ref):
    half = ym_ref.shape[1]
    mixed = (jnp.dot(ym_ref[...], wo_ref[:half, :], preferred_element_type=F32)
             + jnp.dot(yr_ref[...], wo_ref[half:, :], preferred_element_type=F32))
    x1 = x_ref[...] + gt_ref[...] * mixed
    x1_ref[...] = x1
    h2 = _rms(x1, g2_ref[...]) * (1.0 + sc_ref[...]) + sh_ref[...]
    h2_ref[...] = h2
    pq = jnp.dot(h2.astype(BF16), wq_ref[...], preferred_element_type=F32)
    for g in range(2 * PEER_HEADS):
        qg = pq[:, g * PEER_HALF:(g + 1) * PEER_HALF].astype(BF16)
        st_ref[g * PEER_N_KEYS:(g + 1) * PEER_N_KEYS, :] = lax.dot_general(
            keys_ref[g % 2], qg, _NT, preferred_element_type=F32)


def _mix(ym, yr, x2, gt1, sc2, sh2, g2, wo, wq, keys, seq):
    t, d = x2.shape
    tm = 256
    tpb = seq // tm
    row = lambda n: pl.BlockSpec((tm, n), lambda i: (i, 0))
    full = lambda a: pl.BlockSpec(a.shape, lambda i: (0,) * a.ndim)
    mod = pl.BlockSpec((None, 1, d), lambda i: (i // tpb, 0, 0))
    ns = 2 * PEER_HEADS * PEER_N_KEYS
    return pl.pallas_call(
        _mix_kernel,
        grid=(t // tm,),
        in_specs=[row(ym.shape[1]), row(yr.shape[1]), row(d), mod, mod, mod, full(g2), full(wo),
                  full(wq), full(keys)],
        out_specs=[row(d), row(d), pl.BlockSpec((ns, tm), lambda i: (0, i))],
        out_shape=[jax.ShapeDtypeStruct((t, d), F32), jax.ShapeDtypeStruct((t, d), F32),
                   jax.ShapeDtypeStruct((ns, t), F32)],
        compiler_params=_cparams(("parallel",)),
        name="out_proj_peer_scores",
    )(ym, yr, x2, gt1, sc2, sh2, g2, wo, wq, keys)


def _top16(s, payload=None):
    rows = lax.broadcasted_iota(jnp.int32, s.shape, 0).astype(F32)
    vals, sel = [], []
    for _ in range(PEER_TOPK):
        m = jnp.max(s, axis=0, keepdims=True)
        at = jnp.min(jnp.where(s == m, rows, float(s.shape[0])), axis=0, keepdims=True)
        hit = rows == at
        vals.append(m)
        sel.append(at if payload is None else jnp.max(jnp.where(hit, payload, -1.0), axis=0, keepdims=True))
        s = jnp.where(hit, -jnp.inf, s)
    return jnp.concatenate(vals, axis=0), jnp.concatenate(sel, axis=0)


def _pair_grid(r0, r1, combine, fill):
    k = PEER_TOPK
    sub = lax.broadcasted_iota(jnp.int32, (8, r0.shape[1]), 0)
    parts = [combine(r0[0:1], r1), combine(r0[1:2], r1[0:8])]
    for a in range(2, 8):
        parts.append(jnp.where(sub < k // (a + 1), combine(r0[a:a + 1], r1[0:8]), fill))
    parts.append(combine(r0[8:16], r1[0:1]))
    return jnp.concatenate(parts, axis=0)


def _topk_kernel(st_ref, e_ref, g_ref, es_ref, gs_ref):
    nk, k = PEER_N_KEYS, PEER_TOPK

    def head(h, _):
        base = pl.multiple_of(h * 2 * nk, 2 * nk)
        v0, i0 = _top16(st_ref[pl.ds(base, nk), :])
        v1, i1 = _top16(st_ref[pl.ds(base + nk, nk), :])
        cand = _pair_grid(v0, v1, lambda x, y: x + y, -jnp.inf)
        cidx = _pair_grid(i0, i1, lambda x, y: x * float(nk) + y, 0.0)
        best, eidx = _top16(cand, cidx)
        ex = jnp.exp(best - jnp.max(best, axis=0, keepdims=True))
        gate = ex / jnp.sum(ex, axis=0, keepdims=True)
        row = pl.multiple_of(h * k, k)
        es_ref[pl.ds(row, k), :] = eidx.astype(jnp.int32)
        gs_ref[pl.ds(row, k), :] = gate
        return 0

    lax.fori_loop(0, PEER_HEADS, head, 0)
    e_ref[...] = es_ref[...].T * ROWS_PER_EXPERT
    g_ref[...] = gs_ref[...].T


def _topk(st):
    ns, t = st.shape
    tt = 256
    out = pl.BlockSpec((tt, PEER_SLOTS), lambda i: (i, 0))
    return pl.pallas_call(
        _topk_kernel,
        grid=(t // tt,),
        in_specs=[pl.BlockSpec((ns, tt), lambda i: (0, i))],
        out_specs=[out, out],
        out_shape=[jax.ShapeDtypeStruct((t, PEER_SLOTS), jnp.int32),
                   jax.ShapeDtypeStruct((t, PEER_SLOTS), F32)],
        scratch_shapes=[pltpu.VMEM((PEER_SLOTS, tt), jnp.int32), pltpu.VMEM((PEER_SLOTS, tt), F32)],
        compiler_params=_cparams(("parallel",)),
        name="peer_topk",
    )(st)


PEER_TB = 128
ROWS_PER_EXPERT = 4
TILE_ROWS = PEER_SLOTS * ROWS_PER_EXPERT


def _pack_table(tab):
    e, d = tab.shape
    tb = lax.bitcast_convert_type(tab.astype(BF16), jnp.uint16).astype(jnp.uint32)
    tb = tb.reshape(e, 2, ROWS_PER_EXPERT, LANES)
    word = tb[:, 0] | (tb[:, 1] << 16)
    return lax.bitcast_convert_type(word, jnp.int32).reshape(e * ROWS_PER_EXPERT, LANES)


def _peer_layout():
    j = np.arange(2 * TILE_ROWS)
    chunk = (j % 8) // 2 + 4 * (j % 2)
    mask8 = (chunk[None, :] == np.arange(8)[:, None]).astype(np.float32)
    group = (j[:, None] // 8 == np.arange(PEER_SLOTS)[None, :]).astype(np.float32)
    return jnp.asarray(mask8), jnp.asarray(group, BF16), jnp.asarray(group.T, BF16)


def _load_table(tab_hbm, tab_vmem, sem):
    @pl.when(pl.program_id(0) == 0)
    def _():
        cp = pltpu.make_async_copy(tab_hbm, tab_vmem, sem)
        cp.start()
        cp.wait()


PEER_GROUP = 8


def _token_rows(idx_ref, tab_ref, token):
    tok_idx = idx_ref.at[pl.ds(token * PEER_SLOTS, PEER_SLOTS)]
    return [tab_ref[pl.ds(pl.multiple_of(tok_idx[s], ROWS_PER_EXPERT), ROWS_PER_EXPERT), :]
            for s in range(PEER_SLOTS)]


def _split_bf16(a):
    hi = a.astype(BF16)
    lo = (a - hi.astype(F32)).astype(BF16)
    return jnp.concatenate([hi, lo], axis=0)


def _dot_hilo(a, b01):
    hi = a.astype(BF16)
    lo = (a - hi.astype(F32)).astype(BF16)
    return jnp.dot(hi, b01, preferred_element_type=F32) + jnp.dot(lo, b01, preferred_element_type=F32)


def _peer_u_kernel(idx_ref, x_ref, mask_ref, tab_hbm, o_ref, tab_ref, sem):
    _load_table(tab_hbm, tab_ref, sem)
    mask8 = mask_ref[...]

    def group(t8, _):
        rows = []
        for j in range(PEER_GROUP):
            tok = t8 * PEER_GROUP + j
            tile = pltpu.bitcast(jnp.concatenate(_token_rows(idx_ref, tab_ref, tok), axis=0), BF16)
            d = lax.dot_general(_split_bf16(x_ref[tok]), tile, _NT, preferred_element_type=F32)
            rows.append(jnp.sum((d[:8] + d[8:]) * mask8, axis=0, keepdims=True))
        o_ref[pl.ds(pl.multiple_of(t8 * PEER_GROUP, PEER_GROUP), PEER_GROUP), :] = jnp.concatenate(rows, axis=0)
        return 0

    lax.fori_loop(0, PEER_TB // PEER_GROUP, group, 0)


def _peer_v_kernel(idx_ref, a_ref, g_ref, mask_ref, grp_ref, grpt_ref, tab_hbm, o_ref,
                   tab_ref, tile_a, tile_b, wx_ref, sem):
    _load_table(tab_hbm, tab_ref, sem)
    mask8 = mask_ref[...]
    act = _dot_hilo(a_ref[...], grp_ref[...])
    gelu = 0.5 * act * (1.0 + lax.erf(act * math.sqrt(0.5)))
    wx_ref[...] = _dot_hilo(g_ref[...] * gelu, grpt_ref[...])

    def group(t8, _):
        w8 = wx_ref[pl.ds(pl.multiple_of(t8 * PEER_GROUP, PEER_GROUP), PEER_GROUP), :]
        for j in range(PEER_GROUP):
            tok = t8 * PEER_GROUP + j
            tile_ref = tile_a if j % 2 == 0 else tile_b
            for s, slab in enumerate(_token_rows(idx_ref, tab_ref, tok)):
                tile_ref[pl.ds(s * ROWS_PER_EXPERT, ROWS_PER_EXPERT), :] = slab
            lhs = _split_bf16(w8[j:j + 1, :] * mask8)
            out = jnp.dot(lhs, pltpu.bitcast(tile_ref[...], BF16), preferred_element_type=F32)
            o_ref[tok] = out[:8] + out[8:]
        return 0

    lax.fori_loop(0, PEER_TB // PEER_GROUP, group, 0)


def _table_scratch():
    return pltpu.VMEM((PEER_N_KEYS * PEER_N_KEYS * ROWS_PER_EXPERT, LANES), jnp.int32)


def _peer_apply(h2, eidx4, gate, tab_u, tab_v):
    t, d = h2.shape
    tb = PEER_TB
    mask8, grp, grpt = _peer_layout()
    tile = pltpu.VMEM((TILE_ROWS, LANES), jnp.int32)
    idx = eidx4.reshape(t * PEER_SLOTS)
    idx_spec = pl.BlockSpec((tb * PEER_SLOTS,), lambda i: (i,), memory_space=pltpu.SMEM)
    full = lambda a: pl.BlockSpec(a.shape, lambda i: (0,) * a.ndim)
    tok3 = pl.BlockSpec((tb, 8, LANES), lambda i: (i, 0, 0))
    hbm = pl.BlockSpec(memory_space=pl.ANY)
    act = pl.pallas_call(
        _peer_u_kernel,
        grid=(t // tb,),
        in_specs=[idx_spec, tok3, full(mask8), hbm],
        out_specs=pl.BlockSpec((tb, 2 * TILE_ROWS), lambda i: (i, 0)),
        out_shape=jax.ShapeDtypeStruct((t, 2 * TILE_ROWS), F32),
        scratch_shapes=[_table_scratch(), pltpu.SemaphoreType.DMA(())],
        compiler_params=_cparams(("arbitrary",)),
        name="peer_u",
    )(idx, h2.reshape(t, 8, LANES), mask8, tab_u)
    out = pl.pallas_call(
        _peer_v_kernel,
        grid=(t // tb,),
        in_specs=[idx_spec, pl.BlockSpec((tb, 2 * TILE_ROWS), lambda i: (i, 0)),
                  pl.BlockSpec((tb, PEER_SLOTS), lambda i: (i, 0)), full(mask8), full(grp), full(grpt), hbm],
        out_specs=tok3,
        out_shape=jax.ShapeDtypeStruct((t, 8, LANES), F32),
        scratch_shapes=[_table_scratch(), tile, tile, pltpu.VMEM((tb, 2 * TILE_ROWS), F32),
                        pltpu.SemaphoreType.DMA(())],
        compiler_params=_cparams(("arbitrary",)),
        name="peer_v",
    )(idx, act, gate, mask8, grp, grpt, tab_v)
    return out.reshape(t, d)


def _resid_kernel(x_ref, p_ref, gt_ref, g_ref, o_ref, *, final):
    x = x_ref[...] + gt_ref[...] * p_ref[...]
    o_ref[...] = _rms(x, g_ref[...]) if final else x


def _residual(x1, peer, gt2, g_final, seq, final):
    t, d = x1.shape
    tm = 512
    tpb = seq // tm
    row = pl.BlockSpec((tm, d), lambda i: (i, 0))
    return pl.pallas_call(
        functools.partial(_resid_kernel, final=final),
        grid=(t // tm,),
        in_specs=[row, row, pl.BlockSpec((None, 1, d), lambda i: (i // tpb, 0, 0)),
                  pl.BlockSpec((1, d), lambda i: (0, 0))],
        out_specs=row,
        out_shape=jax.ShapeDtypeStruct((t, d), F32),
        compiler_params=_cparams(("parallel",)),
        name="peer_residual",
    )(x1, peer, gt2, g_final.reshape(1, d))


def _mla_head_cols(rope_cols, nope_cols):
    pad = lambda n: [-1] * n
    r1 = list(rope_cols[:16]) if rope_cols is not None else pad(16)
    r2 = list(rope_cols[16:]) if rope_cols is not None else pad(16)
    n1 = list(nope_cols[:48]) if nope_cols is not None else pad(48)
    n2 = list(nope_cols[48:]) if nope_cols is not None else pad(16)
    return r1 + n1 + r2 + n2 + pad(32)


def _take_cols(w, cols):
    cols = np.asarray(cols)
    out = jnp.take(w, jnp.asarray(np.maximum(cols, 0)), axis=1)
    return jnp.where(jnp.asarray(cols >= 0)[None, :], out, 0.0)


def _layer_weights(w_in, w_uq, w_ukv):
    qk = MLA_QK_DIM
    uq_cols, uk_cols, uv_cols = [], [], []
    for h in range(MLA_HEADS):
        uq_cols += _mla_head_cols(range(h * qk + MLA_QK_NOPE, (h + 1) * qk), range(h * qk, h * qk + MLA_QK_NOPE))
        kv0 = h * (MLA_QK_NOPE + MLA_V_DIM)
        uk_cols += _mla_head_cols(None, range(kv0, kv0 + MLA_QK_NOPE))
        uv_cols += list(range(kv0 + MLA_QK_NOPE, kv0 + MLA_QK_NOPE + MLA_V_DIM)) + [-1] * (LANES - MLA_V_DIM)
    o_kr = MLA_Q_RANK + MLA_KV_RANK
    o_r = o_kr + MLA_QK_ROPE
    a_cols = list(range(o_kr)) + _mla_head_cols(range(o_kr, o_r), None)

    def pair_cols(base):
        cols = []
        for p in range(RET_HEADS // 2):
            a, b = base + 2 * p * RET_HEAD_DIM, base + (2 * p + 1) * RET_HEAD_DIM
            cols += list(range(a, a + 32)) + list(range(b, b + 32)) + list(range(a + 32, a + 64)) + list(range(b + 32, b + 64))
        return cols

    r_cols = (pair_cols(o_r) + pair_cols(o_r + RET_WIDTH)
              + list(range(o_r + 2 * RET_WIDTH, o_r + 4 * RET_WIDTH)))
    bf = lambda a: a.astype(BF16)
    return (bf(_take_cols(w_in, a_cols)), bf(_take_cols(w_in, r_cols)), bf(_take_cols(w_uq, uq_cols)),
            bf(_take_cols(w_ukv, uk_cols)), bf(_take_cols(w_ukv, uv_cols)))


def kernel(x, c, positions, w_ada, b_ada, g_norm1, w_in, g_q_norm, w_uq, g_kv_norm, w_ukv, g_ret_norm,
           w_out, g_norm2, w_query, sub_keys, expert_u, expert_v, g_final):
    b, s, d = x.shape
    t = b * s
    depth = w_ada.shape[0]
    tabs = _rope_tables(positions)
    x2 = x.reshape(t, d)
    for l in range(depth):
        mod = _adaln(c, w_ada[l], b_ada[l])
        sh1, sc1, gt1, sh2, sc2, gt2 = [m.reshape(b, 1, d) for m in jnp.split(mod, 6, axis=-1)]
        wa, wr, wuq, wuk, wuv = _layer_weights(w_in[l], w_uq[l], w_ukv[l])
        q, k, v, rq, rk, rv, rg = _proj(
            x2, sc1, sh1, g_norm1[l].reshape(1, d), wa, wr, g_q_norm[l].reshape(1, -1), wuq,
            g_kv_norm[l].reshape(1, -1), wuk, wuv, tabs, s)
        r3 = lambda a: a.reshape(b, s, a.shape[-1])
        y_mla = _attention(r3(q), r3(k), r3(v)).reshape(t, -1)
        y_ret = _retention(r3(rq), r3(rk), r3(rv), r3(rg), g_ret_norm[l]).reshape(t, -1)
        x1, h2, st = _mix(y_mla, y_ret, x2, gt1, sc2, sh2, g_norm2[l].reshape(1, d),
                          w_out[l].astype(BF16), w_query[l].astype(BF16), sub_keys[l].astype(BF16), s)
        eidx4, gate = _topk(st)
        peer = _peer_apply(h2, eidx4, gate, _pack_table(expert_u[l]), _pack_table(expert_v[l]))
        x2 = _residual(x1, peer, gt2, g_final, s, final=(l == depth - 1))
    return x2.reshape(b, s, d)
```

```python
import functools
import math

import jax
import jax.numpy as jnp
import numpy as np
from jax import lax
from jax.experimental import pallas as pl
from jax.experimental.pallas import tpu as pltpu

F32 = jnp.float32
BF16 = jnp.bfloat16
HIGHEST = lax.Precision.HIGHEST

EPS = 1e-6
ROPE_THETA = 10000.0

MLA_HEADS = 8
MLA_QK_NOPE = 64
MLA_QK_ROPE = 32
MLA_QK_DIM = MLA_QK_NOPE + MLA_QK_ROPE
MLA_V_DIM = 64
MLA_Q_RANK = 256
MLA_KV_RANK = 128

RET_HEADS = 8
RET_HEAD_DIM = 64
RET_WIDTH = RET_HEADS * RET_HEAD_DIM
RET_CHUNK = 128

PEER_HEADS = 8
PEER_N_KEYS = 128
PEER_HALF = 128
PEER_TOPK = 16
PEER_SLOTS = PEER_HEADS * PEER_TOPK

LANES = 128
VMEM_LIMIT = 56 * 1024 * 1024

_NT = (((1,), (1,)), ((), ()))


def _cparams(sem):
    return pltpu.CompilerParams(dimension_semantics=sem, vmem_limit_bytes=VMEM_LIMIT)


def _adaln_kernel(c_ref, w_ref, b_ref, o_ref):
    c = c_ref[...]
    s = c * jax.nn.sigmoid(c)
    o_ref[...] = jnp.dot(s, w_ref[...], precision=HIGHEST, preferred_element_type=F32) + b_ref[...]


def _adaln(c, w, b):
    bsz, d = c.shape
    n = w.shape[1]
    return pl.pallas_call(
        _adaln_kernel,
        grid=(n // d,),
        in_specs=[
            pl.BlockSpec((bsz, d), lambda j: (0, 0)),
            pl.BlockSpec((d, d), lambda j: (0, j)),
            pl.BlockSpec((1, d), lambda j: (0, j)),
        ],
        out_specs=pl.BlockSpec((bsz, d), lambda j: (0, j)),
        out_shape=jax.ShapeDtypeStruct((bsz, n), F32),
        compiler_params=_cparams(("arbitrary",)),
        name="adaln",
    )(c, w, b.reshape(1, n))


def _rope_tab_kernel(pos_ref, inv_ref, sg_ref, cr_ref, sr_ref, cm_ref, sm_ref):
    pos = pos_ref[...]
    ang_r = pos * inv_ref[0:1, :]
    ang_m = pos * inv_ref[1:2, :]
    cr_ref[...] = jnp.cos(ang_r)
    sr_ref[...] = jnp.sin(ang_r) * sg_ref[0:1, :]
    cm_ref[...] = jnp.cos(ang_m)
    sm_ref[...] = jnp.sin(ang_m) * sg_ref[1:2, :]


def _rope_tables(positions):
    t = positions.size
    pos = positions.reshape(t, 1).astype(F32)
    inv_r = ROPE_THETA ** (-jnp.arange(0, RET_HEAD_DIM, 2, dtype=F32) / RET_HEAD_DIM)
    inv_m = ROPE_THETA ** (-jnp.arange(0, MLA_QK_ROPE, 2, dtype=F32) / MLA_QK_ROPE)
    z = lambda n: jnp.zeros((n,), F32)
    o = lambda n: jnp.ones((n,), F32)
    inv = jnp.stack([
        jnp.tile(inv_r, 4),
        jnp.concatenate([inv_m, z(48), inv_m, z(48)]),
    ])
    sg = jnp.stack([
        jnp.concatenate([-o(64), o(64)]),
        jnp.concatenate([-o(16), z(48), o(16), z(48)]),
    ])
    tm = 512
    tab = jax.ShapeDtypeStruct((t, LANES), F32)
    spec = pl.BlockSpec((tm, LANES), lambda i: (i, 0))
    cst = pl.BlockSpec((2, LANES), lambda i: (0, 0))
    return pl.pallas_call(
        _rope_tab_kernel,
        grid=(t // tm,),
        in_specs=[pl.BlockSpec((tm, 1), lambda i: (i, 0)), cst, cst],
        out_specs=[spec] * 4,
        out_shape=[tab] * 4,
        compiler_params=_cparams(("parallel",)),
        name="rope_tables",
    )(pos, inv, sg)


def _rms(x, g):
    return x * lax.rsqrt(jnp.mean(x * x, axis=-1, keepdims=True) + EPS) * g


def _rot(x, c, s):
    return x * c + pltpu.roll(x, 64, 1) * s


def _proj_kernel(x_ref, sc_ref, sh_ref, g1_ref, wa_ref, wr_ref, gq_ref, wuq_ref, gkv_ref,
                 wuk_ref, wuv_ref, cr_ref, sr_ref, cm_ref, sm_ref,
                 q_ref, k_ref, v_ref, rq_ref, rk_ref, rv_ref, rg_ref):
    x = x_ref[...]
    h = _rms(x, g1_ref[...]) * (1.0 + sc_ref[...]) + sh_ref[...]
    hb = h.astype(BF16)
    cm, sm = cm_ref[...], sm_ref[...]
    cr, sr = cr_ref[...], sr_ref[...]

    pa = jnp.dot(hb, wa_ref[...], preferred_element_type=F32)
    q_lat = pa[:, :MLA_Q_RANK]
    kv_lat = pa[:, MLA_Q_RANK:MLA_Q_RANK + MLA_KV_RANK]
    kr = _rot(pa[:, MLA_Q_RANK + MLA_KV_RANK:], cm, sm)

    qn = _rms(q_lat, gq_ref[...]).astype(BF16)
    q = jnp.dot(qn, wuq_ref[...], preferred_element_type=F32)
    scale = MLA_QK_DIM ** -0.5 * math.log2(math.e)
    for hd in range(MLA_HEADS):
        sl = slice(hd * LANES, (hd + 1) * LANES)
        q_ref[:, sl] = (_rot(q[:, sl], cm, sm) * scale).astype(BF16)

    kvn = _rms(kv_lat, gkv_ref[...]).astype(BF16)
    k = jnp.dot(kvn, wuk_ref[...], preferred_element_type=F32)
    for hd in range(MLA_HEADS):
        sl = slice(hd * LANES, (hd + 1) * LANES)
        k_ref[:, sl] = (k[:, sl] + kr).astype(BF16)
    v = jnp.dot(kvn, wuv_ref[...], preferred_element_type=F32)
    vlane = lax.broadcasted_iota(jnp.int32, v.shape, 1) & (LANES - 1)
    v_ref[...] = jnp.where(vlane == MLA_V_DIM, 1.0, v).astype(BF16)

    pr = jnp.dot(hb, wr_ref[...], preferred_element_type=F32)
    w = RET_WIDTH
    for p in range(RET_HEADS // 2):
        sl = slice(p * LANES, (p + 1) * LANES)
        rq_ref[:, sl] = _rot(pr[:, p * LANES:(p + 1) * LANES], cr, sr).astype(BF16)
        rk_ref[:, sl] = (_rot(pr[:, w + p * LANES:w + (p + 1) * LANES], cr, sr)
                         * (RET_HEAD_DIM ** -0.5)).astype(BF16)
    rv_ref[...] = pr[:, 2 * w:3 * w].astype(BF16)
    rg_ref[...] = pr[:, 3 * w:]


def _proj(x2, sc1, sh1, g1, wa, wr, gq, wuq, gkv, wuk, wuv, tabs, seq):
    t, d = x2.shape
    tm = 256
    tpb = seq // tm
    cr, sr, cm, sm = tabs
    row = lambda n: pl.BlockSpec((tm, n), lambda i: (i, 0))
    full = lambda a: pl.BlockSpec(a.shape, lambda i: (0,) * a.ndim)
    mod = pl.BlockSpec((None, 1, d), lambda i: (i // tpb, 0, 0))
    outs = [(8 * LANES, BF16), (8 * LANES, BF16), (8 * LANES, BF16), (512, BF16), (512, BF16),
            (512, BF16), (512, F32)]
    return pl.pallas_call(
        _proj_kernel,
        grid=(t // tm,),
        in_specs=[row(d), mod, mod, full(g1), full(wa), full(wr), full(gq), full(wuq), full(gkv),
                  full(wuk), full(wuv), row(LANES), row(LANES), row(LANES), row(LANES)],
        out_specs=[row(n) for n, _ in outs],
        out_shape=[jax.ShapeDtypeStruct((t, n), dt) for n, dt in outs],
        compiler_params=_cparams(("parallel",)),
        name="in_proj",
    )(x2, sc1, sh1, g1, wa, wr, gq, wuq, gkv, wuk, wuv, cr, sr, cm, sm)


def _attn_kernel(q_ref, k_ref, v_ref, o_ref, *, tq, tk):
    i = pl.program_id(2)
    row = lax.broadcasted_iota(jnp.int32, (tq, tk), 0)
    col = lax.broadcasted_iota(jnp.int32, (tq, tk), 1)
    heads = [slice(hh * LANES, (hh + 1) * LANES) for hh in range(2)]
    qs = [q_ref[:, hs] for hs in heads]

    def step(off, carry, mask):
        new = []
        for hs, q, (m, acc) in zip(heads, qs, carry):
            kb = k_ref[pl.ds(off, tk), hs]
            vb = v_ref[pl.ds(off, tk), hs]
            s = lax.dot_general(q, kb, _NT, preferred_element_type=F32)
            if mask is not None:
                s = jnp.where(mask, s, -1e30)
            m_new = jnp.maximum(m, jnp.max(s, axis=1, keepdims=True))
            p = jnp.exp2(s - m_new)
            acc = jnp.exp2(m - m_new) * acc + jnp.dot(p.astype(BF16), vb, preferred_element_type=F32)
            new.append((m_new, acc))
        return tuple(new)

    carry = ((jnp.full((tq, 1), -1e30, F32), jnp.zeros((tq, LANES), F32)),) * 2
    carry = lax.fori_loop(0, i * (tq // tk), lambda j, c: step(pl.multiple_of(j * tk, tk), c, None), carry)
    for dblk in range(tq // tk):
        carry = step(pl.multiple_of(i * tq + dblk * tk, tk), carry, col + dblk * tk <= row)
    outs = [acc[:, :MLA_V_DIM] / acc[:, MLA_V_DIM:MLA_V_DIM + 1] for _, acc in carry]
    o_ref[...] = jnp.concatenate(outs, axis=1).astype(o_ref.dtype)


def _attention(q, k, v):
    b, s, _ = q.shape
    tq, tk = 1024, 1024
    return pl.pallas_call(
        functools.partial(_attn_kernel, tq=tq, tk=tk),
        grid=(b, MLA_HEADS // 2, s // tq),
        in_specs=[
            pl.BlockSpec((None, tq, 2 * LANES), lambda bi, p, i: (bi, i, p)),
            pl.BlockSpec((None, s, 2 * LANES), lambda bi, p, i: (bi, 0, p)),
            pl.BlockSpec((None, s, 2 * LANES), lambda bi, p, i: (bi, 0, p)),
        ],
        out_specs=pl.BlockSpec((None, tq, LANES), lambda bi, p, i: (bi, i, p)),
        out_shape=jax.ShapeDtypeStruct((b, s, MLA_HEADS * MLA_V_DIM), BF16),
        compiler_params=_cparams(("parallel", "parallel", "arbitrary")),
        name="mla_attention",
    )(q, k, v)


def _ret_kernel(q_ref, k_ref, v_ref, g_ref, dm_ref, z_ref, xi_ref, dec_ref, gn_ref, o_ref, st_ref,
                *, nchunk):
    c = RET_CHUNK

    @pl.when(pl.program_id(2) == 0)
    def _():
        st_ref[...] = jnp.zeros_like(st_ref)

    lane = lax.broadcasted_iota(jnp.int32, (c, LANES), 1)
    sub = lax.broadcasted_iota(jnp.int32, (c, LANES), 0)
    v_first = lane < RET_HEAD_DIM
    k_first = (lane & 32) == 0
    same_head = ((sub & 32) == 0) == v_first
    for ci in range(nchunk):
        sl = slice(ci * c, (ci + 1) * c)
        q, k, v = q_ref[sl, :], k_ref[sl, :], v_ref[sl, :]
        zero = jnp.zeros_like(q)
        s_a = lax.dot_general(jnp.where(k_first, q, zero), k, _NT, preferred_element_type=F32) * dm_ref[0]
        s_b = lax.dot_general(jnp.where(k_first, zero, q), k, _NT, preferred_element_type=F32) * dm_ref[1]
        s_ab = jnp.concatenate([s_a, s_b], axis=1).astype(BF16)
        v_bd = jnp.concatenate([jnp.where(v_first, v, zero), jnp.where(v_first, zero, v)], axis=0)
        y = jnp.dot(s_ab, v_bd, preferred_element_type=F32)

        st = st_ref[...]
        st_hi = st.astype(BF16)
        st_lo = (st - st_hi.astype(F32)).astype(BF16)
        y = y + (jnp.dot(q, st_hi, preferred_element_type=F32)
                 + jnp.dot(q, st_lo, preferred_element_type=F32)) * xi_ref[...]

        vz = (v.astype(F32) * z_ref[...]).astype(BF16)
        kt = k.astype(F32).T.astype(BF16)
        kv = jnp.dot(kt, vz, preferred_element_type=F32)
        st_ref[...] = st * dec_ref[...] + jnp.where(same_head, kv, 0.0)

        def head_mean(a):
            tot = jnp.sum(a, axis=1, keepdims=True)
            first = jnp.sum(jnp.where(v_first, a, 0.0), axis=1, keepdims=True)
            return jnp.where(v_first, first, tot - first) * (1.0 / RET_HEAD_DIM)

        dlt = y - head_mean(y)
        yn = dlt * lax.rsqrt(head_mean(dlt * dlt) + EPS) * gn_ref[...]
        gate = g_ref[sl, :]
        o_ref[sl, :] = (gate * jax.nn.sigmoid(gate) * yn).astype(o_ref.dtype)


def _retention_consts():
    h, c = RET_HEADS, RET_CHUNK
    gamma = 1.0 - 2.0 ** (-5.0 - jnp.arange(h, dtype=F32))
    log_g = jnp.log(gamma)
    idx = jnp.arange(c, dtype=F32)
    diff = idx[:, None] - idx[None, :]
    dmask = jnp.where(diff >= 0, jnp.exp(log_g[:, None, None] * jnp.maximum(diff, 0.0)), 0.0)
    zeta = jnp.exp(log_g[:, None] * (c - 1.0 - idx))
    xi = jnp.exp(log_g[:, None] * (idx + 1.0))
    decay = jnp.exp(log_g * c)
    by_lane = lambda a: jnp.repeat(a.reshape(h // 2, 2, -1), RET_HEAD_DIM, axis=1)
    z = by_lane(zeta).transpose(0, 2, 1)
    x = by_lane(xi).transpose(0, 2, 1)
    dec = by_lane(decay[:, None]).transpose(0, 2, 1)
    return dmask, z, x, dec


def _retention(rq, rk, rv, rg, g_ret):
    b, s, w = rq.shape
    tc = 512
    dmask, z, xi, dec = _retention_consts()
    blk = pl.BlockSpec((None, tc, LANES), lambda bi, p, t: (bi, t, p))
    per_pair = lambda shp: pl.BlockSpec((None,) + shp, lambda bi, p, t: (p, 0, 0))
    return pl.pallas_call(
        functools.partial(_ret_kernel, nchunk=tc // RET_CHUNK),
        grid=(b, RET_HEADS // 2, s // tc),
        in_specs=[blk, blk, blk, blk,
                  pl.BlockSpec((2, RET_CHUNK, RET_CHUNK), lambda bi, p, t: (p, 0, 0)),
                  per_pair((RET_CHUNK, LANES)), per_pair((RET_CHUNK, LANES)), per_pair((1, LANES)),
                  pl.BlockSpec((1, LANES), lambda bi, p, t: (0, p))],
        out_specs=blk,
        out_shape=jax.ShapeDtypeStruct((b, s, w), BF16),
        scratch_shapes=[pltpu.VMEM((LANES, LANES), F32)],
        compiler_params=_cparams(("parallel", "parallel", "arbitrary")),
        name="retention",
    )(rq, rk, rv, rg, dmask, z, xi, dec, g_ret.reshape(1, w))


def _mix_kernel(ym_ref, yr_ref, x_ref, gt_ref, sc_ref, sh_ref, g2_ref, wo_ref, wq_ref, keys_ref,
                x1_ref, h2_ref, st_ref):
    half = ym_ref.shape[1]
    mixed = (jnp.dot(ym_ref[...], wo_ref[:half, :], preferred_element_type=F32)
             + jnp.dot(yr_ref[...], wo_ref[half:, :], preferred_element_type=F32))
    x1 = x_ref[...] + gt_ref[...] * mixed
    x1_ref[...] = x1
    h2 = _rms(x1, g2_ref[...]) * (1.0 + sc_ref[...]) + sh_ref[...]
    h2_ref[...] = h2
    pq = jnp.dot(h2.astype(BF16), wq_ref[...], preferred_element_type=F32)
    for g in range(2 * PEER_HEADS):
        qg = pq[:, g * PEER_HALF:(g + 1) * PEER_HALF].astype(BF16)
        st_ref[g * PEER_N_KEYS:(g + 1) * PEER_N_KEYS, :] = lax.dot_general(
            keys_ref[g % 2], qg, _NT, preferred_element_type=F32)


def _mix(ym, yr, x2, gt1, sc2, sh2, g2, wo, wq, keys, seq):
    t, d = x2.shape
    tm = 256
    tpb = seq // tm
    row = lambda n: pl.BlockSpec((tm, n), lambda i: (i, 0))
    full = lambda a: pl.BlockSpec(a.shape, lambda i: (0,) * a.ndim)
    mod = pl.BlockSpec((None, 1, d), lambda i: (i // tpb, 0, 0))
    ns = 2 * PEER_HEADS * PEER_N_KEYS
    return pl.pallas_call(
        _mix_kernel,
        grid=(t // tm,),
        in_specs=[row(ym.shape[1]), row(yr.shape[1]), row(d), mod, mod, mod, full(g2), full(wo),
                  full(wq), full(keys)],
        out_specs=[row(d), row(d), pl.BlockSpec((ns, tm), lambda i: (0, i))],
        out_shape=[jax.ShapeDtypeStruct((t, d), F32), jax.ShapeDtypeStruct((t, d), F32),
                   jax.ShapeDtypeStruct((ns, t), F32)],
        compiler_params=_cparams(("parallel",)),
        name="out_proj_peer_scores",
    )(ym, yr, x2, gt1, sc2, sh2, g2, wo, wq, keys)


def _top16(s, payload=None):
    rows = lax.broadcasted_iota(jnp.int32, s.shape, 0).astype(F32)
    vals, sel = [], []
    for _ in range(PEER_TOPK):
        m = jnp.max(s, axis=0, keepdims=True)
        at = jnp.min(jnp.where(s == m, rows, float(s.shape[0])), axis=0, keepdims=True)
        hit = rows == at
        vals.append(m)
        sel.append(at if payload is None else jnp.max(jnp.where(hit, payload, -1.0), axis=0, keepdims=True))
        s = jnp.where(hit, -jnp.inf, s)
    return jnp.concatenate(vals, axis=0), jnp.concatenate(sel, axis=0)


def _pair_grid(r0, r1, combine, fill):
    k = PEER_TOPK
    sub = lax.broadcasted_iota(jnp.int32, (8, r0.shape[1]), 0)
    parts = [combine(r0[0:1], r1), combine(r0[1:2], r1[0:8])]
    for a in range(2, 8):
        parts.append(jnp.where(sub < k // (a + 1), combine(r0[a:a + 1], r1[0:8]), fill))
    parts.append(combine(r0[8:16], r1[0:1]))
    return jnp.concatenate(parts, axis=0)


def _topk_kernel(st_ref, e_ref, g_ref, es_ref, gs_ref):
    nk, k = PEER_N_KEYS, PEER_TOPK

    def head(h, _):
        base = pl.multiple_of(h * 2 * nk, 2 * nk)
        v0, i0 = _top16(st_ref[pl.ds(base, nk), :])
        v1, i1 = _top16(st_ref[pl.ds(base + nk, nk), :])
        cand = _pair_grid(v0, v1, lambda x, y: x + y, -jnp.inf)
        cidx = _pair_grid(i0, i1, lambda x, y: x * float(nk) + y, 0.0)
        best, eidx = _top16(cand, cidx)
        ex = jnp.exp(best - jnp.max(best, axis=0, keepdims=True))
        gate = ex / jnp.sum(ex, axis=0, keepdims=True)
        row = pl.multiple_of(h * k, k)
        es_ref[pl.ds(row, k), :] = eidx.astype(jnp.int32)
        gs_ref[pl.ds(row, k), :] = gate
        return 0

    lax.fori_loop(0, PEER_HEADS, head, 0)
    e_ref[...] = es_ref[...].T * ROWS_PER_EXPERT
    g_ref[...] = gs_ref[...].T


def _topk(st):
    ns, t = st.shape
    tt = 256
    out = pl.BlockSpec((tt, PEER_SLOTS), lambda i: (i, 0))
    return pl.pallas_call(
        _topk_kernel,
        grid=(t // tt,),
        in_specs=[pl.BlockSpec((ns, tt), lambda i: (0, i))],
        out_specs=[out, out],
        out_shape=[jax.ShapeDtypeStruct((t, PEER_SLOTS), jnp.int32),
                   jax.ShapeDtypeStruct((t, PEER_SLOTS), F32)],
        scratch_shapes=[pltpu.VMEM((PEER_SLOTS, tt), jnp.int32), pltpu.VMEM((PEER_SLOTS, tt), F32)],
        compiler_params=_cparams(("parallel",)),
        name="peer_topk",
    )(st)


PEER_TB = 128
ROWS_PER_EXPERT = 4
TILE_ROWS = PEER_SLOTS * ROWS_PER_EXPERT


def _pack_table(tab):
    e, d = tab.shape
    tb = lax.bitcast_convert_type(tab.astype(BF16), jnp.uint16).astype(jnp.uint32)
    tb = tb.reshape(e, 2, ROWS_PER_EXPERT, LANES)
    word = tb[:, 0] | (tb[:, 1] << 16)
    return lax.bitcast_convert_type(word, jnp.int32).reshape(e * ROWS_PER_EXPERT, LANES)


def _peer_layout():
    j = np.arange(2 * TILE_ROWS)
    chunk = (j % 8) // 2 + 4 * (j % 2)
    mask8 = (chunk[None, :] == np.arange(8)[:, None]).astype(np.float32)
    group = (j[:, None] // 8 == np.arange(PEER_SLOTS)[None, :]).astype(np.float32)
    return jnp.asarray(mask8), jnp.asarray(group, BF16), jnp.asarray(group.T, BF16)


def _load_table(tab_hbm, tab_vmem, sem):
    @pl.when(pl.program_id(0) == 0)
    def _():
        cp = pltpu.make_async_copy(tab_hbm, tab_vmem, sem)
        cp.start()
        cp.wait()


PEER_GROUP = 8


def _token_tile(idx_ref, tab_ref, token):
    tok_idx = idx_ref.at[pl.ds(token * PEER_SLOTS, PEER_SLOTS)]
    slabs = [tab_ref[pl.ds(pl.multiple_of(tok_idx[s], ROWS_PER_EXPERT), ROWS_PER_EXPERT), :]
             for s in range(PEER_SLOTS)]
    return pltpu.bitcast(jnp.concatenate(slabs, axis=0), BF16)


def _split_bf16(a):
    hi = a.astype(BF16)
    lo = (a - hi.astype(F32)).astype(BF16)
    return jnp.concatenate([hi, lo], axis=0)


def _dot_hilo(a, b01):
    hi = a.astype(BF16)
    lo = (a - hi.astype(F32)).astype(BF16)
    return jnp.dot(hi, b01, preferred_element_type=F32) + jnp.dot(lo, b01, preferred_element_type=F32)


def _peer_u_kernel(idx_ref, x_ref, mask_ref, tab_hbm, o_ref, tab_ref, sem):
    _load_table(tab_hbm, tab_ref, sem)
    mask8 = mask_ref[...]

    def group(t8, _):
        rows8 = pl.ds(pl.multiple_of(t8 * PEER_GROUP, PEER_GROUP), PEER_GROUP)
        x8 = x_ref[rows8, :].reshape(PEER_GROUP, 8, LANES)
        rows = []
        for j in range(PEER_GROUP):
            tile = _token_tile(idx_ref, tab_ref, t8 * PEER_GROUP + j)
            d = lax.dot_general(_split_bf16(x8[j]), tile, _NT, preferred_element_type=F32)
            rows.append(jnp.sum((d[:8] + d[8:]) * mask8, axis=0, keepdims=True))
        o_ref[rows8, :] = jnp.concatenate(rows, axis=0)
        return 0

    lax.fori_loop(0, PEER_TB // PEER_GROUP, group, 0)


def _peer_v_kernel(idx_ref, a_ref, g_ref, mask_ref, grp_ref, grpt_ref, tab_hbm, o_ref,
                   tab_ref, wx_ref, sem):
    _load_table(tab_hbm, tab_ref, sem)
    mask8 = mask_ref[...]
    act = _dot_hilo(a_ref[...], grp_ref[...])
    gelu = 0.5 * act * (1.0 + lax.erf(act * math.sqrt(0.5)))
    wx_ref[...] = _dot_hilo(g_ref[...] * gelu, grpt_ref[...])

    def group(t8, _):
        rows8 = pl.ds(pl.multiple_of(t8 * PEER_GROUP, PEER_GROUP), PEER_GROUP)
        w8 = wx_ref[rows8, :]
        outs = []
        for j in range(PEER_GROUP):
            lhs = _split_bf16(w8[j:j + 1, :] * mask8)
            tile = _token_tile(idx_ref, tab_ref, t8 * PEER_GROUP + j)
            out = jnp.dot(lhs, tile, preferred_element_type=F32)
            outs.append(out[:8] + out[8:])
        o_ref[rows8, :] = jnp.stack(outs, axis=0).reshape(PEER_GROUP, 8 * LANES)
        return 0

    lax.fori_loop(0, PEER_TB // PEER_GROUP, group, 0)


def _table_scratch():
    return pltpu.VMEM((PEER_N_KEYS * PEER_N_KEYS * ROWS_PER_EXPERT, LANES), jnp.int32)


def _peer_apply(h2, eidx4, gate, tab_u, tab_v):
    t, d = h2.shape
    tb = PEER_TB
    mask8, grp, grpt = _peer_layout()
    idx = eidx4.reshape(t * PEER_SLOTS)
    idx_spec = pl.BlockSpec((tb * PEER_SLOTS,), lambda i: (i,), memory_space=pltpu.SMEM)
    full = lambda a: pl.BlockSpec(a.shape, lambda i: (0,) * a.ndim)
    tok_rows = pl.BlockSpec((tb, d), lambda i: (i, 0))
    hbm = pl.BlockSpec(memory_space=pl.ANY)
    act = pl.pallas_call(
        _peer_u_kernel,
        grid=(t // tb,),
        in_specs=[idx_spec, tok_rows, full(mask8), hbm],
        out_specs=pl.BlockSpec((tb, 2 * TILE_ROWS), lambda i: (i, 0)),
        out_shape=jax.ShapeDtypeStruct((t, 2 * TILE_ROWS), F32),
        scratch_shapes=[_table_scratch(), pltpu.SemaphoreType.DMA(())],
        compiler_params=_cparams(("arbitrary",)),
        name="peer_u",
    )(idx, h2, mask8, tab_u)
    out = pl.pallas_call(
        _peer_v_kernel,
        grid=(t // tb,),
        in_specs=[idx_spec, pl.BlockSpec((tb, 2 * TILE_ROWS), lambda i: (i, 0)),
                  pl.BlockSpec((tb, PEER_SLOTS), lambda i: (i, 0)), full(mask8), full(grp), full(grpt), hbm],
        out_specs=tok_rows,
        out_shape=jax.ShapeDtypeStruct((t, d), F32),
        scratch_shapes=[_table_scratch(), pltpu.VMEM((tb, 2 * TILE_ROWS), F32),
                        pltpu.SemaphoreType.DMA(())],
        compiler_params=_cparams(("arbitrary",)),
        name="peer_v",
    )(idx, act, gate, mask8, grp, grpt, tab_v)
    return out


def _resid_kernel(x_ref, p_ref, gt_ref, g_ref, o_ref, *, final):
    x = x_ref[...] + gt_ref[...] * p_ref[...]
    o_ref[...] = _rms(x, g_ref[...]) if final else x


def _residual(x1, peer, gt2, g_final, seq, final):
    t, d = x1.shape
    tm = 512
    tpb = seq // tm
    row = pl.BlockSpec((tm, d), lambda i: (i, 0))
    return pl.pallas_call(
        functools.partial(_resid_kernel, final=final),
        grid=(t // tm,),
        in_specs=[row, row, pl.BlockSpec((None, 1, d), lambda i: (i // tpb, 0, 0)),
                  pl.BlockSpec((1, d), lambda i: (0, 0))],
        out_specs=row,
        out_shape=jax.ShapeDtypeStruct((t, d), F32),
        compiler_params=_cparams(("parallel",)),
        name="peer_residual",
    )(x1, peer, gt2, g_final.reshape(1, d))


def _mla_head_cols(rope_cols, nope_cols):
    pad = lambda n: [-1] * n
    r1 = list(rope_cols[:16]) if rope_cols is not None else pad(16)
    r2 = list(rope_cols[16:]) if rope_cols is not None else pad(16)
    n1 = list(nope_cols[:48]) if nope_cols is not None else pad(48)
    n2 = list(nope_cols[48:]) if nope_cols is not None else pad(16)
    return r1 + n1 + r2 + n2 + pad(32)


def _take_cols(w, cols):
    cols = np.asarray(cols)
    out = jnp.take(w, jnp.asarray(np.maximum(cols, 0)), axis=1)
    return jnp.where(jnp.asarray(cols >= 0)[None, :], out, 0.0)


def _layer_weights(w_in, w_uq, w_ukv):
    qk = MLA_QK_DIM
    uq_cols, uk_cols, uv_cols = [], [], []
    for h in range(MLA_HEADS):
        uq_cols += _mla_head_cols(range(h * qk + MLA_QK_NOPE, (h + 1) * qk), range(h * qk, h * qk + MLA_QK_NOPE))
        kv0 = h * (MLA_QK_NOPE + MLA_V_DIM)
        uk_cols += _mla_head_cols(None, range(kv0, kv0 + MLA_QK_NOPE))
        uv_cols += list(range(kv0 + MLA_QK_NOPE, kv0 + MLA_QK_NOPE + MLA_V_DIM)) + [-1] * (LANES - MLA_V_DIM)
    o_kr = MLA_Q_RANK + MLA_KV_RANK
    o_r = o_kr + MLA_QK_ROPE
    a_cols = list(range(o_kr)) + _mla_head_cols(range(o_kr, o_r), None)

    def pair_cols(base):
        cols = []
        for p in range(RET_HEADS // 2):
            a, b = base + 2 * p * RET_HEAD_DIM, base + (2 * p + 1) * RET_HEAD_DIM
            cols += list(range(a, a + 32)) + list(range(b, b + 32)) + list(range(a + 32, a + 64)) + list(range(b + 32, b + 64))
        return cols

    r_cols = (pair_cols(o_r) + pair_cols(o_r + RET_WIDTH)
              + list(range(o_r + 2 * RET_WIDTH, o_r + 4 * RET_WIDTH)))
    bf = lambda a: a.astype(BF16)
    return (bf(_take_cols(w_in, a_cols)), bf(_take_cols(w_in, r_cols)), bf(_take_cols(w_uq, uq_cols)),
            bf(_take_cols(w_ukv, uk_cols)), bf(_take_cols(w_ukv, uv_cols)))


def kernel(x, c, positions, w_ada, b_ada, g_norm1, w_in, g_q_norm, w_uq, g_kv_norm, w_ukv, g_ret_norm,
           w_out, g_norm2, w_query, sub_keys, expert_u, expert_v, g_final):
    b, s, d = x.shape
    t = b * s
    depth = w_ada.shape[0]
    tabs = _rope_tables(positions)
    x2 = x.reshape(t, d)
    for l in range(depth):
        mod = _adaln(c, w_ada[l], b_ada[l])
        sh1, sc1, gt1, sh2, sc2, gt2 = [m.reshape(b, 1, d) for m in jnp.split(mod, 6, axis=-1)]
        wa, wr, wuq, wuk, wuv = _layer_weights(w_in[l], w_uq[l], w_ukv[l])
        q, k, v, rq, rk, rv, rg = _proj(
            x2, sc1, sh1, g_norm1[l].reshape(1, d), wa, wr, g_q_norm[l].reshape(1, -1), wuq,
            g_kv_norm[l].reshape(1, -1), wuk, wuv, tabs, s)
        r3 = lambda a: a.reshape(b, s, a.shape[-1])
        y_mla = _attention(r3(q), r3(k), r3(v)).reshape(t, -1)
        y_ret = _retention(r3(rq), r3(rk), r3(rv), r3(rg), g_ret_norm[l]).reshape(t, -1)
        x1, h2, st = _mix(y_mla, y_ret, x2, gt1, sc2, sh2, g_norm2[l].reshape(1, d),
                          w_out[l].astype(BF16), w_query[l].astype(BF16), sub_keys[l].astype(BF16), s)
        eidx4, gate = _topk(st)
        peer = _peer_apply(h2, eidx4, gate, _pack_table(expert_u[l]), _pack_table(expert_v[l]))
        x2 = _residual(x1, peer, gt2, g_final, s, final=(l == depth - 1))
    return x2.reshape(b, s, d)
```

```python
import functools
import math

import jax
import jax.numpy as jnp
import numpy as np
from jax import lax
from jax.experimental import pallas as pl
from jax.experimental.pallas import tpu as pltpu

F32 = jnp.float32
BF16 = jnp.bfloat16
HIGHEST = lax.Precision.HIGHEST

EPS = 1e-6
ROPE_THETA = 10000.0

MLA_HEADS = 8
MLA_QK_NOPE = 64
MLA_QK_ROPE = 32
MLA_QK_DIM = MLA_QK_NOPE + MLA_QK_ROPE
MLA_V_DIM = 64
MLA_Q_RANK = 256
MLA_KV_RANK = 128

RET_HEADS = 8
RET_HEAD_DIM = 64
RET_WIDTH = RET_HEADS * RET_HEAD_DIM
RET_CHUNK = 128

PEER_HEADS = 8
PEER_N_KEYS = 128
PEER_HALF = 128
PEER_TOPK = 16
PEER_SLOTS = PEER_HEADS * PEER_TOPK

LANES = 128
VMEM_LIMIT = 56 * 1024 * 1024

_NT = (((1,), (1,)), ((), ()))


def _cparams(sem):
    return pltpu.CompilerParams(dimension_semantics=sem, vmem_limit_bytes=VMEM_LIMIT)


def _adaln_kernel(c_ref, w_ref, b_ref, o_ref):
    c = c_ref[...]
    s = c * jax.nn.sigmoid(c)
    o_ref[...] = jnp.dot(s, w_ref[...], precision=HIGHEST, preferred_element_type=F32) + b_ref[...]


def _adaln(c, w, b):
    bsz, d = c.shape
    n = w.shape[1]
    return pl.pallas_call(
        _adaln_kernel,
        grid=(n // d,),
        in_specs=[
            pl.BlockSpec((bsz, d), lambda j: (0, 0)),
            pl.BlockSpec((d, d), lambda j: (0, j)),
            pl.BlockSpec((1, d), lambda j: (0, j)),
        ],
        out_specs=pl.BlockSpec((bsz, d), lambda j: (0, j)),
        out_shape=jax.ShapeDtypeStruct((bsz, n), F32),
        compiler_params=_cparams(("arbitrary",)),
        name="adaln",
    )(c, w, b.reshape(1, n))


def _rope_tab_kernel(pos_ref, inv_ref, sg_ref, cr_ref, sr_ref, cm_ref, sm_ref):
    pos = pos_ref[...]
    ang_r = pos * inv_ref[0:1, :]
    ang_m = pos * inv_ref[1:2, :]
    cr_ref[...] = jnp.cos(ang_r)
    sr_ref[...] = jnp.sin(ang_r) * sg_ref[0:1, :]
    cm_ref[...] = jnp.cos(ang_m)
    sm_ref[...] = jnp.sin(ang_m) * sg_ref[1:2, :]


def _rope_tables(positions):
    t = positions.size
    pos = positions.reshape(t, 1).astype(F32)
    inv_r = ROPE_THETA ** (-jnp.arange(0, RET_HEAD_DIM, 2, dtype=F32) / RET_HEAD_DIM)
    inv_m = ROPE_THETA ** (-jnp.arange(0, MLA_QK_ROPE, 2, dtype=F32) / MLA_QK_ROPE)
    z = lambda n: jnp.zeros((n,), F32)
    o = lambda n: jnp.ones((n,), F32)
    inv = jnp.stack([
        jnp.tile(inv_r, 4),
        jnp.concatenate([inv_m, z(48), inv_m, z(48)]),
    ])
    sg = jnp.stack([
        jnp.concatenate([-o(64), o(64)]),
        jnp.concatenate([-o(16), z(48), o(16), z(48)]),
    ])
    tm = 512
    tab = jax.ShapeDtypeStruct((t, LANES), F32)
    spec = pl.BlockSpec((tm, LANES), lambda i: (i, 0))
    cst = pl.BlockSpec((2, LANES), lambda i: (0, 0))
    return pl.pallas_call(
        _rope_tab_kernel,
        grid=(t // tm,),
        in_specs=[pl.BlockSpec((tm, 1), lambda i: (i, 0)), cst, cst],
        out_specs=[spec] * 4,
        out_shape=[tab] * 4,
        compiler_params=_cparams(("parallel",)),
        name="rope_tables",
    )(pos, inv, sg)


def _rms(x, g):
    return x * lax.rsqrt(jnp.mean(x * x, axis=-1, keepdims=True) + EPS) * g


def _rot(x, c, s):
    return x * c + pltpu.roll(x, 64, 1) * s


def _proj_kernel(x_ref, sc_ref, sh_ref, g1_ref, wa_ref, wr_ref, gq_ref, wuq_ref, gkv_ref,
                 wuk_ref, wuv_ref, cr_ref, sr_ref, cm_ref, sm_ref,
                 q_ref, k_ref, v_ref, rq_ref, rk_ref, rv_ref, rg_ref):
    x = x_ref[...]
    h = _rms(x, g1_ref[...]) * (1.0 + sc_ref[...]) + sh_ref[...]
    hb = h.astype(BF16)
    cm, sm = cm_ref[...], sm_ref[...]
    cr, sr = cr_ref[...], sr_ref[...]

    pa = jnp.dot(hb, wa_ref[...], preferred_element_type=F32)
    q_lat = pa[:, :MLA_Q_RANK]
    kv_lat = pa[:, MLA_Q_RANK:MLA_Q_RANK + MLA_KV_RANK]
    kr = _rot(pa[:, MLA_Q_RANK + MLA_KV_RANK:], cm, sm)

    qn = _rms(q_lat, gq_ref[...]).astype(BF16)
    q = jnp.dot(qn, wuq_ref[...], preferred_element_type=F32)
    scale = MLA_QK_DIM ** -0.5 * math.log2(math.e)
    for hd in range(MLA_HEADS):
        sl = slice(hd * LANES, (hd + 1) * LANES)
        q_ref[:, sl] = (_rot(q[:, sl], cm, sm) * scale).astype(BF16)

    kvn = _rms(kv_lat, gkv_ref[...]).astype(BF16)
    k = jnp.dot(kvn, wuk_ref[...], preferred_element_type=F32)
    for hd in range(MLA_HEADS):
        sl = slice(hd * LANES, (hd + 1) * LANES)
        k_ref[:, sl] = (k[:, sl] + kr).astype(BF16)
    v = jnp.dot(kvn, wuv_ref[...], preferred_element_type=F32)
    vlane = lax.broadcasted_iota(jnp.int32, v.shape, 1) & (LANES - 1)
    v_ref[...] = jnp.where(vlane == MLA_V_DIM, 1.0, v).astype(BF16)

    pr = jnp.dot(hb, wr_ref[...], preferred_element_type=F32)
    w = RET_WIDTH
    for p in range(RET_HEADS // 2):
        sl = slice(p * LANES, (p + 1) * LANES)
        rq_ref[:, sl] = _rot(pr[:, p * LANES:(p + 1) * LANES], cr, sr).astype(BF16)
        rk_ref[:, sl] = (_rot(pr[:, w + p * LANES:w + (p + 1) * LANES], cr, sr)
                         * (RET_HEAD_DIM ** -0.5)).astype(BF16)
    rv_ref[...] = pr[:, 2 * w:3 * w].astype(BF16)
    rg_ref[...] = pr[:, 3 * w:]


def _proj(x2, sc1, sh1, g1, wa, wr, gq, wuq, gkv, wuk, wuv, tabs, seq):
    t, d = x2.shape
    tm = 256
    tpb = seq // tm
    cr, sr, cm, sm = tabs
    row = lambda n: pl.BlockSpec((tm, n), lambda i: (i, 0))
    full = lambda a: pl.BlockSpec(a.shape, lambda i: (0,) * a.ndim)
    mod = pl.BlockSpec((None, 1, d), lambda i: (i // tpb, 0, 0))
    outs = [(8 * LANES, BF16), (8 * LANES, BF16), (8 * LANES, BF16), (512, BF16), (512, BF16),
            (512, BF16), (512, F32)]
    return pl.pallas_call(
        _proj_kernel,
        grid=(t // tm,),
        in_specs=[row(d), mod, mod, full(g1), full(wa), full(wr), full(gq), full(wuq), full(gkv),
                  full(wuk), full(wuv), row(LANES), row(LANES), row(LANES), row(LANES)],
        out_specs=[row(n) for n, _ in outs],
        out_shape=[jax.ShapeDtypeStruct((t, n), dt) for n, dt in outs],
        compiler_params=_cparams(("parallel",)),
        name="in_proj",
    )(x2, sc1, sh1, g1, wa, wr, gq, wuq, gkv, wuk, wuv, cr, sr, cm, sm)


def _softmax_update(q, kb, vb, m, acc, mask=None):
    s = lax.dot_general(q, kb, _NT, preferred_element_type=F32)
    if mask is not None:
        s = jnp.where(mask, s, -1e30)
    m_new = jnp.maximum(m, jnp.max(s, axis=1, keepdims=True))
    p = jnp.exp2(s - m_new)
    acc = jnp.exp2(m - m_new) * acc + jnp.dot(p.astype(BF16), vb, preferred_element_type=F32)
    return m_new, acc


def _attn_kernel(q_ref, k_ref, v_ref, o_ref, *, tq):
    i = pl.program_id(2)
    half = tq // 2
    heads = [slice(hh * LANES, (hh + 1) * LANES) for hh in range(2)]
    qs = [q_ref[:, hs] for hs in heads]

    def full_block(j, carry):
        off = pl.multiple_of(j * tq, tq)
        return tuple(_softmax_update(q, k_ref[pl.ds(off, tq), hs], v_ref[pl.ds(off, tq), hs], m, acc)
                     for hs, q, (m, acc) in zip(heads, qs, carry))

    carry = ((jnp.full((tq, 1), -1e30, F32), jnp.zeros((tq, LANES), F32)),) * 2
    carry = lax.fori_loop(0, i, full_block, carry)

    off = pl.multiple_of(i * tq, tq)
    tri = (lax.broadcasted_iota(jnp.int32, (half, half), 1) <= lax.broadcasted_iota(jnp.int32, (half, half), 0))
    low = (lax.broadcasted_iota(jnp.int32, (half, tq), 1) <= lax.broadcasted_iota(jnp.int32, (half, tq), 0) + half)
    outs = []
    for hs, q, (m, acc) in zip(heads, qs, carry):
        m0, a0 = _softmax_update(q[:half], k_ref[pl.ds(off, half), hs], v_ref[pl.ds(off, half), hs],
                                 m[:half], acc[:half], tri)
        m1, a1 = _softmax_update(q[half:], k_ref[pl.ds(off, tq), hs], v_ref[pl.ds(off, tq), hs],
                                 m[half:], acc[half:], low)
        acc = jnp.concatenate([a0, a1], axis=0)
        outs.append(acc[:, :MLA_V_DIM] / acc[:, MLA_V_DIM:MLA_V_DIM + 1])
    o_ref[...] = jnp.concatenate(outs, axis=1).astype(o_ref.dtype)


def _attention(q, k, v):
    b, s, _ = q.shape
    tq = 1024
    return pl.pallas_call(
        functools.partial(_attn_kernel, tq=tq),
        grid=(b, MLA_HEADS // 2, s // tq),
        in_specs=[
            pl.BlockSpec((None, tq, 2 * LANES), lambda bi, p, i: (bi, i, p)),
            pl.BlockSpec((None, s, 2 * LANES), lambda bi, p, i: (bi, 0, p)),
            pl.BlockSpec((None, s, 2 * LANES), lambda bi, p, i: (bi, 0, p)),
        ],
        out_specs=pl.BlockSpec((None, tq, LANES), lambda bi, p, i: (bi, i, p)),
        out_shape=jax.ShapeDtypeStruct((b, s, MLA_HEADS * MLA_V_DIM), BF16),
        compiler_params=_cparams(("parallel", "parallel", "arbitrary")),
        name="mla_attention",
    )(q, k, v)


def _ret_kernel(q_ref, k_ref, v_ref, g_ref, dm_ref, z_ref, xi_ref, dec_ref, gn_ref, o_ref, st_ref,
                *, nchunk):
    c = RET_CHUNK

    @pl.when(pl.program_id(2) == 0)
    def _():
        st_ref[...] = jnp.zeros_like(st_ref)

    lane = lax.broadcasted_iota(jnp.int32, (c, LANES), 1)
    sub = lax.broadcasted_iota(jnp.int32, (c, LANES), 0)
    v_first = lane < RET_HEAD_DIM
    k_first = (lane & 32) == 0
    same_head = ((sub & 32) == 0) == v_first
    for ci in range(nchunk):
        sl = slice(ci * c, (ci + 1) * c)
        q, k, v = q_ref[sl, :], k_ref[sl, :], v_ref[sl, :]
        zero = jnp.zeros_like(q)
        s_a = lax.dot_general(jnp.where(k_first, q, zero), k, _NT, preferred_element_type=F32) * dm_ref[0]
        s_b = lax.dot_general(jnp.where(k_first, zero, q), k, _NT, preferred_element_type=F32) * dm_ref[1]
        s_ab = jnp.concatenate([s_a, s_b], axis=1).astype(BF16)
        v_bd = jnp.concatenate([jnp.where(v_first, v, zero), jnp.where(v_first, zero, v)], axis=0)
        y = jnp.dot(s_ab, v_bd, preferred_element_type=F32)

        st = st_ref[...]
        st_hi = st.astype(BF16)
        st_lo = (st - st_hi.astype(F32)).astype(BF16)
        y = y + (jnp.dot(q, st_hi, preferred_element_type=F32)
                 + jnp.dot(q, st_lo, preferred_element_type=F32)) * xi_ref[...]

        vz = (v.astype(F32) * z_ref[...]).astype(BF16)
        kt = k.astype(F32).T.astype(BF16)
        kv = jnp.dot(kt, vz, preferred_element_type=F32)
        st_ref[...] = st * dec_ref[...] + jnp.where(same_head, kv, 0.0)

        def head_mean(a):
            tot = jnp.sum(a, axis=1, keepdims=True)
            first = jnp.sum(jnp.where(v_first, a, 0.0), axis=1, keepdims=True)
            return jnp.where(v_first, first, tot - first) * (1.0 / RET_HEAD_DIM)

        dlt = y - head_mean(y)
        yn = dlt * lax.rsqrt(head_mean(dlt * dlt) + EPS) * gn_ref[...]
        gate = g_ref[sl, :]
        o_ref[sl, :] = (gate * jax.nn.sigmoid(gate) * yn).astype(o_ref.dtype)


def _retention_consts():
    h, c = RET_HEADS, RET_CHUNK
    gamma = 1.0 - 2.0 ** (-5.0 - jnp.arange(h, dtype=F32))
    log_g = jnp.log(gamma)
    idx = jnp.arange(c, dtype=F32)
    diff = idx[:, None] - idx[None, :]
    dmask = jnp.where(diff >= 0, jnp.exp(log_g[:, None, None] * jnp.maximum(diff, 0.0)), 0.0)
    zeta = jnp.exp(log_g[:, None] * (c - 1.0 - idx))
    xi = jnp.exp(log_g[:, None] * (idx + 1.0))
    decay = jnp.exp(log_g * c)
    by_lane = lambda a: jnp.repeat(a.reshape(h // 2, 2, -1), RET_HEAD_DIM, axis=1)
    z = by_lane(zeta).transpose(0, 2, 1)
    x = by_lane(xi).transpose(0, 2, 1)
    dec = by_lane(decay[:, None]).transpose(0, 2, 1)
    return dmask, z, x, dec


def _retention(rq, rk, rv, rg, g_ret):
    b, s, w = rq.shape
    tc = 512
    dmask, z, xi, dec = _retention_consts()
    blk = pl.BlockSpec((None, tc, LANES), lambda bi, p, t: (bi, t, p))
    per_pair = lambda shp: pl.BlockSpec((None,) + shp, lambda bi, p, t: (p, 0, 0))
    return pl.pallas_call(
        functools.partial(_ret_kernel, nchunk=tc // RET_CHUNK),
        grid=(b, RET_HEADS // 2, s // tc),
        in_specs=[blk, blk, blk, blk,
                  pl.BlockSpec((2, RET_CHUNK, RET_CHUNK), lambda bi, p, t: (p, 0, 0)),
                  per_pair((RET_CHUNK, LANES)), per_pair((RET_CHUNK, LANES)), per_pair((1, LANES)),
                  pl.BlockSpec((1, LANES), lambda bi, p, t: (0, p))],
        out_specs=blk,
        out_shape=jax.ShapeDtypeStruct((b, s, w), BF16),
        scratch_shapes=[pltpu.VMEM((LANES, LANES), F32)],
        compiler_params=_cparams(("parallel", "parallel", "arbitrary")),
        name="retention",
    )(rq, rk, rv, rg, dmask, z, xi, dec, g_ret.reshape(1, w))


def _mix_kernel(ym_ref, yr_ref, x_ref, gt_ref, sc_ref, sh_ref, g2_ref, wo_ref, wq_ref, keys_ref,
                x1_ref, h2_ref, st_ref):
    half = ym_ref.shape[1]
    mixed = (jnp.dot(ym_ref[...], wo_ref[:half, :], preferred_element_type=F32)
             + jnp.dot(yr_ref[...], wo_ref[half:, :], preferred_element_type=F32))
    x1 = x_ref[...] + gt_ref[...] * mixed
    x1_ref[...] = x1
    h2 = _rms(x1, g2_ref[...]) * (1.0 + sc_ref[...]) + sh_ref[...]
    h2_ref[...] = h2
    pq = jnp.dot(h2.astype(BF16), wq_ref[...], preferred_element_type=F32)
    for g in range(2 * PEER_HEADS):
        qg = pq[:, g * PEER_HALF:(g + 1) * PEER_HALF].astype(BF16)
        st_ref[g * PEER_N_KEYS:(g + 1) * PEER_N_KEYS, :] = lax.dot_general(
            keys_ref[g % 2], qg, _NT, preferred_element_type=F32)


def _mix(ym, yr, x2, gt1, sc2, sh2, g2, wo, wq, keys, seq):
    t, d = x2.shape
    tm = 256
    tpb = seq // tm
    row = lambda n: pl.BlockSpec((tm, n), lambda i: (i, 0))
    full = lambda a: pl.BlockSpec(a.shape, lambda i: (0,) * a.ndim)
    mod = pl.BlockSpec((None, 1, d), lambda i: (i // tpb, 0, 0))
    ns = 2 * PEER_HEADS * PEER_N_KEYS
    return pl.pallas_call(
        _mix_kernel,
        grid=(t // tm,),
        in_specs=[row(ym.shape[1]), row(yr.shape[1]), row(d), mod, mod, mod, full(g2), full(wo),
                  full(wq), full(keys)],
        out_specs=[row(d), row(d), pl.BlockSpec((ns, tm), lambda i: (0, i))],
        out_shape=[jax.ShapeDtypeStruct((t, d), F32), jax.ShapeDtypeStruct((t, d), F32),
                   jax.ShapeDtypeStruct((ns, t), F32)],
        compiler_params=_cparams(("parallel",)),
        name="out_proj_peer_scores",
    )(ym, yr, x2, gt1, sc2, sh2, g2, wo, wq, keys)


def _top16(s, payload=None):
    rows = lax.broadcasted_iota(jnp.int32, s.shape, 0).astype(F32)
    vals, sel = [], []
    for _ in range(PEER_TOPK):
        m = jnp.max(s, axis=0, keepdims=True)
        at = jnp.min(jnp.where(s == m, rows, float(s.shape[0])), axis=0, keepdims=True)
        hit = rows == at
        vals.append(m)
        sel.append(at if payload is None else jnp.max(jnp.where(hit, payload, -1.0), axis=0, keepdims=True))
        s = jnp.where(hit, -jnp.inf, s)
    return jnp.concatenate(vals, axis=0), jnp.concatenate(sel, axis=0)


def _pair_grid(r0, r1, combine, fill):
    k = PEER_TOPK
    sub = lax.broadcasted_iota(jnp.int32, (8, r0.shape[1]), 0)
    parts = [combine(r0[0:1], r1), combine(r0[1:2], r1[0:8])]
    for a in range(2, 8):
        parts.append(jnp.where(sub < k // (a + 1), combine(r0[a:a + 1], r1[0:8]), fill))
    parts.append(combine(r0[8:16], r1[0:1]))
    return jnp.concatenate(parts, axis=0)


def _topk_kernel(st_ref, e_ref, g_ref, es_ref, gs_ref):
    nk, k = PEER_N_KEYS, PEER_TOPK

    def head(h, _):
        base = pl.multiple_of(h * 2 * nk, 2 * nk)
        v0, i0 = _top16(st_ref[pl.ds(base, nk), :])
        v1, i1 = _top16(st_ref[pl.ds(base + nk, nk), :])
        cand = _pair_grid(v0, v1, lambda x, y: x + y, -jnp.inf)
        cidx = _pair_grid(i0, i1, lambda x, y: x * float(nk) + y, 0.0)
        best, eidx = _top16(cand, cidx)
        ex = jnp.exp(best - jnp.max(best, axis=0, keepdims=True))
        gate = ex / jnp.sum(ex, axis=0, keepdims=True)
        row = pl.multiple_of(h * k, k)
        es_ref[pl.ds(row, k), :] = eidx.astype(jnp.int32)
        gs_ref[pl.ds(row, k), :] = gate
        return 0

    lax.fori_loop(0, PEER_HEADS, head, 0)
    e_ref[...] = es_ref[...].T * ROWS_PER_EXPERT
    g_ref[...] = gs_ref[...].T


def _topk(st):
    ns, t = st.shape
    tt = 256
    out = pl.BlockSpec((tt, PEER_SLOTS), lambda i: (i, 0))
    return pl.pallas_call(
        _topk_kernel,
        grid=(t // tt,),
        in_specs=[pl.BlockSpec((ns, tt), lambda i: (0, i))],
        out_specs=[out, out],
        out_shape=[jax.ShapeDtypeStruct((t, PEER_SLOTS), jnp.int32),
                   jax.ShapeDtypeStruct((t, PEER_SLOTS), F32)],
        scratch_shapes=[pltpu.VMEM((PEER_SLOTS, tt), jnp.int32), pltpu.VMEM((PEER_SLOTS, tt), F32)],
        compiler_params=_cparams(("parallel",)),
        name="peer_topk",
    )(st)


PEER_TB = 128
ROWS_PER_EXPERT = 4
TILE_ROWS = PEER_SLOTS * ROWS_PER_EXPERT


def _pack_kernel(x_ref, o_ref):
    bits = pltpu.bitcast(x_ref[...], jnp.uint32)
    rne = bits + jnp.uint32(0x7FFF) + ((bits >> 16) & jnp.uint32(1))
    half = x_ref.shape[1] // 2
    word = (rne[:, :half] >> 16) | (rne[:, half:] & jnp.uint32(0xFFFF0000))
    o_ref[...] = pltpu.bitcast(word, jnp.int32).reshape(o_ref.shape)


def _pack_table(tab):
    e, d = tab.shape
    te = 256
    return pl.pallas_call(
        _pack_kernel,
        grid=(e // te,),
        in_specs=[pl.BlockSpec((te, d), lambda i: (i, 0))],
        out_specs=pl.BlockSpec((te * ROWS_PER_EXPERT, LANES), lambda i: (i, 0)),
        out_shape=jax.ShapeDtypeStruct((e * ROWS_PER_EXPERT, LANES), jnp.int32),
        compiler_params=_cparams(("parallel",)),
        name="pack_table",
    )(tab)


def _peer_layout():
    j = np.arange(2 * TILE_ROWS)
    chunk = (j % 8) // 2 + 4 * (j % 2)
    mask8 = (chunk[None, :] == np.arange(8)[:, None]).astype(np.float32)
    group = (j[:, None] // 8 == np.arange(PEER_SLOTS)[None, :]).astype(np.float32)
    return jnp.asarray(mask8), jnp.asarray(group, BF16), jnp.asarray(group.T, BF16)


def _load_table(tab_hbm, tab_vmem, sem):
    @pl.when(pl.program_id(0) == 0)
    def _():
        cp = pltpu.make_async_copy(tab_hbm, tab_vmem, sem)
        cp.start()
        cp.wait()


PEER_GROUP = 8


def _token_tile(idx_ref, tab_ref, token):
    tok_idx = idx_ref.at[pl.ds(token * PEER_SLOTS, PEER_SLOTS)]
    slabs = [tab_ref[pl.ds(pl.multiple_of(tok_idx[s], ROWS_PER_EXPERT), ROWS_PER_EXPERT), :]
             for s in range(PEER_SLOTS)]
    return pltpu.bitcast(jnp.concatenate(slabs, axis=0), BF16)


def _split_bf16(a):
    hi = a.astype(BF16)
    lo = (a - hi.astype(F32)).astype(BF16)
    return jnp.concatenate([hi, lo], axis=0)


def _dot_hilo(a, b01):
    hi = a.astype(BF16)
    lo = (a - hi.astype(F32)).astype(BF16)
    return jnp.dot(hi, b01, preferred_element_type=F32) + jnp.dot(lo, b01, preferred_element_type=F32)


def _peer_u_kernel(idx_ref, x_ref, mask_ref, tab_hbm, o_ref, tab_ref, sem):
    _load_table(tab_hbm, tab_ref, sem)
    mask8 = mask_ref[...]

    def group(t8, _):
        rows8 = pl.ds(pl.multiple_of(t8 * PEER_GROUP, PEER_GROUP), PEER_GROUP)
        x8 = x_ref[rows8, :].reshape(PEER_GROUP, 8, LANES)
        rows = []
        for j in range(PEER_GROUP):
            tile = _token_tile(idx_ref, tab_ref, t8 * PEER_GROUP + j)
            d = lax.dot_general(_split_bf16(x8[j]), tile, _NT, preferred_element_type=F32)
            rows.append(jnp.sum((d[:8] + d[8:]) * mask8, axis=0, keepdims=True))
        o_ref[rows8, :] = jnp.concatenate(rows, axis=0)
        return 0

    lax.fori_loop(0, PEER_TB // PEER_GROUP, group, 0)


def _peer_v_kernel(idx_ref, a_ref, g_ref, mask_ref, grp_ref, grpt_ref, x1_ref, gt_ref, gfin_ref, tab_hbm, o_ref,
                   tab_ref, wx_ref, sem, *, final):
    _load_table(tab_hbm, tab_ref, sem)
    mask8 = mask_ref[...]
    act = _dot_hilo(a_ref[...], grp_ref[...])
    gelu = 0.5 * act * (1.0 + lax.erf(act * math.sqrt(0.5)))
    wx_ref[...] = _dot_hilo(g_ref[...] * gelu, grpt_ref[...])

    def group(t8, _):
        rows8 = pl.ds(pl.multiple_of(t8 * PEER_GROUP, PEER_GROUP), PEER_GROUP)
        w8 = wx_ref[rows8, :]
        outs = []
        for j in range(PEER_GROUP):
            lhs = _split_bf16(w8[j:j + 1, :] * mask8)
            tile = _token_tile(idx_ref, tab_ref, t8 * PEER_GROUP + j)
            out = jnp.dot(lhs, tile, preferred_element_type=F32)
            outs.append(out[:8] + out[8:])
        o_ref[rows8, :] = jnp.stack(outs, axis=0).reshape(PEER_GROUP, 8 * LANES)
        return 0

    lax.fori_loop(0, PEER_TB // PEER_GROUP, group, 0)
    x = x1_ref[...] + gt_ref[...] * o_ref[...]
    o_ref[...] = _rms(x, gfin_ref[...]) if final else x


def _table_scratch():
    return pltpu.VMEM((PEER_N_KEYS * PEER_N_KEYS * ROWS_PER_EXPERT, LANES), jnp.int32)


def _peer_apply(h2, eidx4, gate, tab_u, tab_v, x1, gt2, g_final, seq, final):
    t, d = h2.shape
    tb = PEER_TB
    mask8, grp, grpt = _peer_layout()
    idx = eidx4.reshape(t * PEER_SLOTS)
    idx_spec = pl.BlockSpec((tb * PEER_SLOTS,), lambda i: (i,), memory_space=pltpu.SMEM)
    full = lambda a: pl.BlockSpec(a.shape, lambda i: (0,) * a.ndim)
    tok_rows = pl.BlockSpec((tb, d), lambda i: (i, 0))
    hbm = pl.BlockSpec(memory_space=pl.ANY)
    act = pl.pallas_call(
        _peer_u_kernel,
        grid=(t // tb,),
        in_specs=[idx_spec, tok_rows, full(mask8), hbm],
        out_specs=pl.BlockSpec((tb, 2 * TILE_ROWS), lambda i: (i, 0)),
        out_shape=jax.ShapeDtypeStruct((t, 2 * TILE_ROWS), F32),
        scratch_shapes=[_table_scratch(), pltpu.SemaphoreType.DMA(())],
        compiler_params=_cparams(("arbitrary",)),
        name="peer_u",
    )(idx, h2, mask8, tab_u)
    tpb = seq // tb
    gfin = g_final.reshape(1, d)
    out = pl.pallas_call(
        functools.partial(_peer_v_kernel, final=final),
        grid=(t // tb,),
        in_specs=[idx_spec, pl.BlockSpec((tb, 2 * TILE_ROWS), lambda i: (i, 0)),
                  pl.BlockSpec((tb, PEER_SLOTS), lambda i: (i, 0)), full(mask8), full(grp), full(grpt),
                  tok_rows, pl.BlockSpec((None, 1, d), lambda i: (i // tpb, 0, 0)), full(gfin), hbm],
        out_specs=tok_rows,
        out_shape=jax.ShapeDtypeStruct((t, d), F32),
        scratch_shapes=[_table_scratch(), pltpu.VMEM((tb, 2 * TILE_ROWS), F32),
                        pltpu.SemaphoreType.DMA(())],
        compiler_params=_cparams(("arbitrary",)),
        name="peer_v",
    )(idx, act, gate, mask8, grp, grpt, x1, gt2, gfin, tab_v)
    return out


def _mla_head_cols(rope_cols, nope_cols):
    pad = lambda n: [-1] * n
    r1 = list(rope_cols[:16]) if rope_cols is not None else pad(16)
    r2 = list(rope_cols[16:]) if rope_cols is not None else pad(16)
    n1 = list(nope_cols[:48]) if nope_cols is not None else pad(48)
    n2 = list(nope_cols[48:]) if nope_cols is not None else pad(16)
    return r1 + n1 + r2 + n2 + pad(32)


def _take_cols(w, cols):
    cols = np.asarray(cols)
    out = jnp.take(w, jnp.asarray(np.maximum(cols, 0)), axis=1)
    return jnp.where(jnp.asarray(cols >= 0)[None, :], out, 0.0)


def _layer_weights(w_in, w_uq, w_ukv):
    qk = MLA_QK_DIM
    uq_cols, uk_cols, uv_cols = [], [], []
    for h in range(MLA_HEADS):
        uq_cols += _mla_head_cols(range(h * qk + MLA_QK_NOPE, (h + 1) * qk), range(h * qk, h * qk + MLA_QK_NOPE))
        kv0 = h * (MLA_QK_NOPE + MLA_V_DIM)
        uk_cols += _mla_head_cols(None, range(kv0, kv0 + MLA_QK_NOPE))
        uv_cols += list(range(kv0 + MLA_QK_NOPE, kv0 + MLA_QK_NOPE + MLA_V_DIM)) + [-1] * (LANES - MLA_V_DIM)
    o_kr = MLA_Q_RANK + MLA_KV_RANK
    o_r = o_kr + MLA_QK_ROPE
    a_cols = list(range(o_kr)) + _mla_head_cols(range(o_kr, o_r), None)

    def pair_cols(base):
        cols = []
        for p in range(RET_HEADS // 2):
            a, b = base + 2 * p * RET_HEAD_DIM, base + (2 * p + 1) * RET_HEAD_DIM
            cols += list(range(a, a + 32)) + list(range(b, b + 32)) + list(range(a + 32, a + 64)) + list(range(b + 32, b + 64))
        return cols

    r_cols = (pair_cols(o_r) + pair_cols(o_r + RET_WIDTH)
              + list(range(o_r + 2 * RET_WIDTH, o_r + 4 * RET_WIDTH)))
    bf = lambda a: a.astype(BF16)
    return (bf(_take_cols(w_in, a_cols)), bf(_take_cols(w_in, r_cols)), bf(_take_cols(w_uq, uq_cols)),
            bf(_take_cols(w_ukv, uk_cols)), bf(_take_cols(w_ukv, uv_cols)))


def kernel(x, c, positions, w_ada, b_ada, g_norm1, w_in, g_q_norm, w_uq, g_kv_norm, w_ukv, g_ret_norm,
           w_out, g_norm2, w_query, sub_keys, expert_u, expert_v, g_final):
    b, s, d = x.shape
    t = b * s
    depth = w_ada.shape[0]
    tabs = _rope_tables(positions)
    x2 = x.reshape(t, d)
    for l in range(depth):
        mod = _adaln(c, w_ada[l], b_ada[l])
        sh1, sc1, gt1, sh2, sc2, gt2 = [m.reshape(b, 1, d) for m in jnp.split(mod, 6, axis=-1)]
        wa, wr, wuq, wuk, wuv = _layer_weights(w_in[l], w_uq[l], w_ukv[l])
        q, k, v, rq, rk, rv, rg = _proj(
            x2, sc1, sh1, g_norm1[l].reshape(1, d), wa, wr, g_q_norm[l].reshape(1, -1), wuq,
            g_kv_norm[l].reshape(1, -1), wuk, wuv, tabs, s)
        r3 = lambda a: a.reshape(b, s, a.shape[-1])
        y_mla = _attention(r3(q), r3(k), r3(v)).reshape(t, -1)
        y_ret = _retention(r3(rq), r3(rk), r3(rv), r3(rg), g_ret_norm[l]).reshape(t, -1)
        x1, h2, st = _mix(y_mla, y_ret, x2, gt1, sc2, sh2, g_norm2[l].reshape(1, d),
                          w_out[l].astype(BF16), w_query[l].astype(BF16), sub_keys[l].astype(BF16), s)
        eidx4, gate = _topk(st)
        x2 = _peer_apply(h2, eidx4, gate, _pack_table(expert_u[l]), _pack_table(expert_v[l]),
                         x1, gt2, g_final, s, final=(l == depth - 1))
    return x2.reshape(b, s, d)
```

```python
import functools
import math

import jax
import jax.numpy as jnp
import numpy as np
from jax import lax
from jax.experimental import pallas as pl
from jax.experimental.pallas import tpu as pltpu

F32 = jnp.float32
BF16 = jnp.bfloat16
HIGHEST = lax.Precision.HIGHEST

EPS = 1e-6
ROPE_THETA = 10000.0

MLA_HEADS = 8
MLA_QK_NOPE = 64
MLA_QK_ROPE = 32
MLA_QK_DIM = MLA_QK_NOPE + MLA_QK_ROPE
MLA_V_DIM = 64
MLA_Q_RANK = 256
MLA_KV_RANK = 128

RET_HEADS = 8
RET_HEAD_DIM = 64
RET_WIDTH = RET_HEADS * RET_HEAD_DIM
RET_CHUNK = 128

PEER_HEADS = 8
PEER_N_KEYS = 128
PEER_HALF = 128
PEER_TOPK = 16
PEER_SLOTS = PEER_HEADS * PEER_TOPK

LANES = 128
VMEM_LIMIT = 56 * 1024 * 1024

_NT = (((1,), (1,)), ((), ()))


def _cparams(sem):
    return pltpu.CompilerParams(dimension_semantics=sem, vmem_limit_bytes=VMEM_LIMIT)


def _adaln_kernel(c_ref, w_ref, b_ref, o_ref):
    c = c_ref[...]
    s = c * jax.nn.sigmoid(c)
    o_ref[...] = jnp.dot(s, w_ref[...], precision=HIGHEST, preferred_element_type=F32) + b_ref[...]


def _adaln(c, w, b):
    bsz, d = c.shape
    n = w.shape[1]
    return pl.pallas_call(
        _adaln_kernel,
        grid=(n // d,),
        in_specs=[
            pl.BlockSpec((bsz, d), lambda j: (0, 0)),
            pl.BlockSpec((d, d), lambda j: (0, j)),
            pl.BlockSpec((1, d), lambda j: (0, j)),
        ],
        out_specs=pl.BlockSpec((bsz, d), lambda j: (0, j)),
        out_shape=jax.ShapeDtypeStruct((bsz, n), F32),
        compiler_params=_cparams(("arbitrary",)),
        name="adaln",
    )(c, w, b.reshape(1, n))


def _rope_tab_kernel(pos_ref, inv_ref, sg_ref, cr_ref, sr_ref, cm_ref, sm_ref):
    ang = pos_ref[...] * inv_ref[...]
    cos, sin = jnp.cos(ang), jnp.sin(ang)
    nr, nm = RET_HEAD_DIM // 2, MLA_QK_ROPE // 2
    ret = lambda a: jnp.concatenate([a[:, :nr]] * (LANES // nr), axis=1)
    mla = lambda a, fill: jnp.concatenate(
        [a[:, nr:nr + nm], jnp.full((a.shape[0], 64 - nm), fill, F32)] * 2, axis=1)
    cr_ref[...] = ret(cos)
    sr_ref[...] = ret(sin) * sg_ref[0:1, :]
    cm_ref[...] = mla(cos, 1.0)
    sm_ref[...] = mla(sin, 0.0) * sg_ref[1:2, :]


def _rope_tables(positions):
    t = positions.size
    pos = positions.reshape(t, 1).astype(F32)
    inv_r = ROPE_THETA ** (-jnp.arange(0, RET_HEAD_DIM, 2, dtype=F32) / RET_HEAD_DIM)
    inv_m = ROPE_THETA ** (-jnp.arange(0, MLA_QK_ROPE, 2, dtype=F32) / MLA_QK_ROPE)
    z = lambda n: jnp.zeros((n,), F32)
    o = lambda n: jnp.ones((n,), F32)
    inv = jnp.concatenate([inv_r, inv_m, z(LANES - 48)]).reshape(1, LANES)
    sg = jnp.stack([
        jnp.concatenate([-o(64), o(64)]),
        jnp.concatenate([-o(16), z(48), o(16), z(48)]),
    ])
    tm = 512
    tab = jax.ShapeDtypeStruct((t, LANES), F32)
    spec = pl.BlockSpec((tm, LANES), lambda i: (i, 0))
    return pl.pallas_call(
        _rope_tab_kernel,
        grid=(t // tm,),
        in_specs=[pl.BlockSpec((tm, 1), lambda i: (i, 0)), pl.BlockSpec((1, LANES), lambda i: (0, 0)),
                  pl.BlockSpec((2, LANES), lambda i: (0, 0))],
        out_specs=[spec] * 4,
        out_shape=[tab] * 4,
        compiler_params=_cparams(("parallel",)),
        name="rope_tables",
    )(pos, inv, sg)


def _rms(x, g):
    return x * lax.rsqrt(jnp.mean(x * x, axis=-1, keepdims=True) + EPS) * g


def _rot(x, c, s):
    return x * c + pltpu.roll(x, 64, 1) * s


def _proj_kernel(x_ref, sc_ref, sh_ref, g1_ref, wa_ref, wr_ref, gq_ref, wuq_ref, gkv_ref,
                 wuk_ref, wuv_ref, cr_ref, sr_ref, cm_ref, sm_ref,
                 q_ref, k_ref, v_ref, rq_ref, rk_ref, rv_ref, rg_ref):
    x = x_ref[...]
    h = _rms(x, g1_ref[...]) * (1.0 + sc_ref[...]) + sh_ref[...]
    hb = h.astype(BF16)
    cm, sm = cm_ref[...], sm_ref[...]
    cr, sr = cr_ref[...], sr_ref[...]

    pa = jnp.dot(hb, wa_ref[...], preferred_element_type=F32)
    q_lat = pa[:, :MLA_Q_RANK]
    kv_lat = pa[:, MLA_Q_RANK:MLA_Q_RANK + MLA_KV_RANK]
    kr = _rot(pa[:, MLA_Q_RANK + MLA_KV_RANK:], cm, sm)

    qn = _rms(q_lat, gq_ref[...]).astype(BF16)
    q = jnp.dot(qn, wuq_ref[...], preferred_element_type=F32)
    scale = MLA_QK_DIM ** -0.5 * math.log2(math.e)
    for hd in range(MLA_HEADS):
        sl = slice(hd * LANES, (hd + 1) * LANES)
        q_ref[:, sl] = (_rot(q[:, sl], cm, sm) * scale).astype(BF16)

    kvn = _rms(kv_lat, gkv_ref[...]).astype(BF16)
    k = jnp.dot(kvn, wuk_ref[...], preferred_element_type=F32)
    for hd in range(MLA_HEADS):
        sl = slice(hd * LANES, (hd + 1) * LANES)
        k_ref[:, sl] = (k[:, sl] + kr).astype(BF16)
    v = jnp.dot(kvn, wuv_ref[...], preferred_element_type=F32)
    vlane = lax.broadcasted_iota(jnp.int32, v.shape, 1) & (LANES - 1)
    v_ref[...] = jnp.where(vlane == MLA_V_DIM, 1.0, v).astype(BF16)

    pr = jnp.dot(hb, wr_ref[...], preferred_element_type=F32)
    w = RET_WIDTH
    for p in range(RET_HEADS // 2):
        sl = slice(p * LANES, (p + 1) * LANES)
        rq_ref[:, sl] = _rot(pr[:, p * LANES:(p + 1) * LANES], cr, sr).astype(BF16)
        rk_ref[:, sl] = (_rot(pr[:, w + p * LANES:w + (p + 1) * LANES], cr, sr)
                         * (RET_HEAD_DIM ** -0.5)).astype(BF16)
    rv_ref[...] = pr[:, 2 * w:3 * w].astype(BF16)
    rg_ref[...] = pr[:, 3 * w:]


def _proj(x2, sc1, sh1, g1, wa, wr, gq, wuq, gkv, wuk, wuv, tabs, seq):
    t, d = x2.shape
    tm = 512
    tpb = seq // tm
    cr, sr, cm, sm = tabs
    row = lambda n: pl.BlockSpec((tm, n), lambda i: (i, 0))
    full = lambda a: pl.BlockSpec(a.shape, lambda i: (0,) * a.ndim)
    mod = pl.BlockSpec((None, 1, d), lambda i: (i // tpb, 0, 0))
    outs = [(8 * LANES, BF16), (8 * LANES, BF16), (8 * LANES, BF16), (512, BF16), (512, BF16),
            (512, BF16), (512, F32)]
    return pl.pallas_call(
        _proj_kernel,
        grid=(t // tm,),
        in_specs=[row(d), mod, mod, full(g1), full(wa), full(wr), full(gq), full(wuq), full(gkv),
                  full(wuk), full(wuv), row(LANES), row(LANES), row(LANES), row(LANES)],
        out_specs=[row(n) for n, _ in outs],
        out_shape=[jax.ShapeDtypeStruct((t, n), dt) for n, dt in outs],
        compiler_params=_cparams(("parallel",)),
        name="in_proj",
    )(x2, sc1, sh1, g1, wa, wr, gq, wuq, gkv, wuk, wuv, cr, sr, cm, sm)


def _softmax_update(q, kb, vb, m, acc, mask=None):
    s = lax.dot_general(q, kb, _NT, preferred_element_type=F32)
    if mask is not None:
        s = jnp.where(mask, s, -1e30)
    m_new = jnp.maximum(m, jnp.max(s, axis=1, keepdims=True))
    p = jnp.exp2(s - m_new)
    acc = jnp.exp2(m - m_new) * acc + jnp.dot(p.astype(BF16), vb, preferred_element_type=F32)
    return m_new, acc


def _attn_kernel(q_ref, k_ref, v_ref, o_ref, *, tq):
    i = pl.program_id(2)
    half = tq // 2
    heads = [slice(hh * LANES, (hh + 1) * LANES) for hh in range(2)]
    qs = [q_ref[:, hs] for hs in heads]

    def full_block(j, carry):
        off = pl.multiple_of(j * tq, tq)
        return tuple(_softmax_update(q, k_ref[pl.ds(off, tq), hs], v_ref[pl.ds(off, tq), hs], m, acc)
                     for hs, q, (m, acc) in zip(heads, qs, carry))

    carry = ((jnp.full((tq, 1), -1e30, F32), jnp.zeros((tq, LANES), F32)),) * 2
    carry = lax.fori_loop(0, i, full_block, carry)

    off = pl.multiple_of(i * tq, tq)
    tri = (lax.broadcasted_iota(jnp.int32, (half, half), 1) <= lax.broadcasted_iota(jnp.int32, (half, half), 0))
    low = (lax.broadcasted_iota(jnp.int32, (half, tq), 1) <= lax.broadcasted_iota(jnp.int32, (half, tq), 0) + half)
    outs = []
    for hs, q, (m, acc) in zip(heads, qs, carry):
        m0, a0 = _softmax_update(q[:half], k_ref[pl.ds(off, half), hs], v_ref[pl.ds(off, half), hs],
                                 m[:half], acc[:half], tri)
        m1, a1 = _softmax_update(q[half:], k_ref[pl.ds(off, tq), hs], v_ref[pl.ds(off, tq), hs],
                                 m[half:], acc[half:], low)
        acc = jnp.concatenate([a0, a1], axis=0)
        outs.append(acc[:, :MLA_V_DIM] / acc[:, MLA_V_DIM:MLA_V_DIM + 1])
    o_ref[...] = jnp.concatenate(outs, axis=1).astype(o_ref.dtype)


def _attention(q, k, v):
    b, s, _ = q.shape
    tq = 1024
    return pl.pallas_call(
        functools.partial(_attn_kernel, tq=tq),
        grid=(b, MLA_HEADS // 2, s // tq),
        in_specs=[
            pl.BlockSpec((None, tq, 2 * LANES), lambda bi, p, i: (bi, i, p)),
            pl.BlockSpec((None, s, 2 * LANES), lambda bi, p, i: (bi, 0, p)),
            pl.BlockSpec((None, s, 2 * LANES), lambda bi, p, i: (bi, 0, p)),
        ],
        out_specs=pl.BlockSpec((None, tq, LANES), lambda bi, p, i: (bi, i, p)),
        out_shape=jax.ShapeDtypeStruct((b, s, MLA_HEADS * MLA_V_DIM), BF16),
        compiler_params=_cparams(("parallel", "parallel", "arbitrary")),
        name="mla_attention",
    )(q, k, v)


def _ret_kernel(q_ref, k_ref, v_ref, g_ref, dm_ref, z_ref, xi_ref, dec_ref, gn_ref, o_ref, st_ref,
                *, nchunk):
    c = RET_CHUNK

    @pl.when(pl.program_id(1) == 0)
    def _():
        st_ref[...] = jnp.zeros_like(st_ref)

    lane = lax.broadcasted_iota(jnp.int32, (c, LANES), 1)
    sub = lax.broadcasted_iota(jnp.int32, (c, LANES), 0)
    v_first = lane < RET_HEAD_DIM
    k_first = (lane & 32) == 0
    same_head = ((sub & 32) == 0) == v_first

    def head_mean(a):
        tot = jnp.sum(a, axis=1, keepdims=True)
        first = jnp.sum(jnp.where(v_first, a, 0.0), axis=1, keepdims=True)
        return jnp.where(v_first, first, tot - first) * (1.0 / RET_HEAD_DIM)

    for ci in range(nchunk):
        sl = slice(ci * c, (ci + 1) * c)
        for p in range(RET_HEADS // 2):
            ps = slice(p * LANES, (p + 1) * LANES)
            q, k, v = q_ref[sl, ps], k_ref[sl, ps], v_ref[sl, ps]
            zero = jnp.zeros_like(q)
            s_a = lax.dot_general(jnp.where(k_first, q, zero), k, _NT, preferred_element_type=F32) * dm_ref[2 * p]
            s_b = lax.dot_general(jnp.where(k_first, zero, q), k, _NT, preferred_element_type=F32) * dm_ref[2 * p + 1]
            s_ab = jnp.concatenate([s_a, s_b], axis=1).astype(BF16)
            v_bd = jnp.concatenate([jnp.where(v_first, v, zero), jnp.where(v_first, zero, v)], axis=0)
            y = jnp.dot(s_ab, v_bd, preferred_element_type=F32)

            st = st_ref[p]
            st_hi = st.astype(BF16)
            st_lo = (st - st_hi.astype(F32)).astype(BF16)
            y = y + (jnp.dot(q, st_hi, preferred_element_type=F32)
                     + jnp.dot(q, st_lo, preferred_element_type=F32)) * xi_ref[p]

            vz = (v.astype(F32) * z_ref[p]).astype(BF16)
            kt = k.astype(F32).T.astype(BF16)
            kv = jnp.dot(kt, vz, preferred_element_type=F32)
            st_ref[p] = st * dec_ref[p] + jnp.where(same_head, kv, 0.0)

            dlt = y - head_mean(y)
            yn = dlt * lax.rsqrt(head_mean(dlt * dlt) + EPS) * gn_ref[:, ps]
            gate = g_ref[sl, ps]
            o_ref[sl, ps] = (gate * jax.nn.sigmoid(gate) * yn).astype(o_ref.dtype)


def _retention_consts():
    h, c = RET_HEADS, RET_CHUNK
    gamma = 1.0 - 2.0 ** (-5.0 - jnp.arange(h, dtype=F32))
    log_g = jnp.log(gamma)
    idx = jnp.arange(c, dtype=F32)
    diff = idx[:, None] - idx[None, :]
    dmask = jnp.where(diff >= 0, jnp.exp(log_g[:, None, None] * jnp.maximum(diff, 0.0)), 0.0)
    zeta = jnp.exp(log_g[:, None] * (c - 1.0 - idx))
    xi = jnp.exp(log_g[:, None] * (idx + 1.0))
    decay = jnp.exp(log_g * c)
    by_lane = lambda a: jnp.repeat(a.reshape(h // 2, 2, -1), RET_HEAD_DIM, axis=1)
    z = by_lane(zeta).transpose(0, 2, 1)
    x = by_lane(xi).transpose(0, 2, 1)
    dec = by_lane(decay[:, None]).transpose(0, 2, 1)
    return dmask, z, x, dec


def _retention(rq, rk, rv, rg, g_ret):
    b, s, w = rq.shape
    tc = 512
    consts = _retention_consts()
    blk = pl.BlockSpec((None, tc, w), lambda bi, t: (bi, t, 0))
    full = lambda a: pl.BlockSpec(a.shape, lambda bi, t: (0,) * a.ndim)
    gn = g_ret.reshape(1, w)
    return pl.pallas_call(
        functools.partial(_ret_kernel, nchunk=tc // RET_CHUNK),
        grid=(b, s // tc),
        in_specs=[blk, blk, blk, blk] + [full(a) for a in consts] + [full(gn)],
        out_specs=blk,
        out_shape=jax.ShapeDtypeStruct((b, s, w), BF16),
        scratch_shapes=[pltpu.VMEM((RET_HEADS // 2, LANES, LANES), F32)],
        compiler_params=_cparams(("parallel", "arbitrary")),
        name="retention",
    )(rq, rk, rv, rg, *consts, gn)


def _mix_kernel(ym_ref, yr_ref, x_ref, gt_ref, sc_ref, sh_ref, g2_ref, wo_ref, wq_ref, keys_ref,
                x1_ref, h2_ref, st_ref):
    half = ym_ref.shape[1]
    mixed = (jnp.dot(ym_ref[...], wo_ref[:half, :], preferred_element_type=F32)
             + jnp.dot(yr_ref[...], wo_ref[half:, :], preferred_element_type=F32))
    x1 = x_ref[...] + gt_ref[...] * mixed
    x1_ref[...] = x1
    h2 = _rms(x1, g2_ref[...]) * (1.0 + sc_ref[...]) + sh_ref[...]
    h2_ref[...] = h2
    pq = jnp.dot(h2.astype(BF16), wq_ref[...], preferred_element_type=F32)
    for g in range(2 * PEER_HEADS):
        qg = pq[:, g * PEER_HALF:(g + 1) * PEER_HALF].astype(BF16)
        st_ref[g * PEER_N_KEYS:(g + 1) * PEER_N_KEYS, :] = lax.dot_general(
            keys_ref[g % 2], qg, _NT, preferred_element_type=F32)


def _mix(ym, yr, x2, gt1, sc2, sh2, g2, wo, wq, keys, seq):
    t, d = x2.shape
    tm = 512
    tpb = seq // tm
    row = lambda n: pl.BlockSpec((tm, n), lambda i: (i, 0))
    full = lambda a: pl.BlockSpec(a.shape, lambda i: (0,) * a.ndim)
    mod = pl.BlockSpec((None, 1, d), lambda i: (i // tpb, 0, 0))
    ns = 2 * PEER_HEADS * PEER_N_KEYS
    return pl.pallas_call(
        _mix_kernel,
        grid=(t // tm,),
        in_specs=[row(ym.shape[1]), row(yr.shape[1]), row(d), mod, mod, mod, full(g2), full(wo),
                  full(wq), full(keys)],
        out_specs=[row(d), row(d), pl.BlockSpec((ns, tm), lambda i: (0, i))],
        out_shape=[jax.ShapeDtypeStruct((t, d), F32), jax.ShapeDtypeStruct((t, d), F32),
                   jax.ShapeDtypeStruct((ns, t), F32)],
        compiler_params=_cparams(("parallel",)),
        name="out_proj_peer_scores",
    )(ym, yr, x2, gt1, sc2, sh2, g2, wo, wq, keys)


def _top16(s, payload=None):
    rows = lax.broadcasted_iota(jnp.int32, s.shape, 0).astype(F32)
    vals, sel = [], []
    for _ in range(PEER_TOPK):
        m = jnp.max(s, axis=0, keepdims=True)
        at = jnp.min(jnp.where(s == m, rows, float(s.shape[0])), axis=0, keepdims=True)
        hit = rows == at
        vals.append(m)
        sel.append(at if payload is None else jnp.max(jnp.where(hit, payload, -1.0), axis=0, keepdims=True))
        s = jnp.where(hit, -jnp.inf, s)
    return jnp.concatenate(vals, axis=0), jnp.concatenate(sel, axis=0)


def _pair_grid(r0, r1, combine, fill):
    k = PEER_TOPK
    sub = lax.broadcasted_iota(jnp.int32, (8, r0.shape[1]), 0)
    parts = [combine(r0[0:1], r1), combine(r0[1:2], r1[0:8])]
    for a in range(2, 8):
        parts.append(jnp.where(sub < k // (a + 1), combine(r0[a:a + 1], r1[0:8]), fill))
    parts.append(combine(r0[8:16], r1[0:1]))
    return jnp.concatenate(parts, axis=0)


def _topk_kernel(st_ref, e_ref, g_ref, es_ref, gs_ref):
    nk, k = PEER_N_KEYS, PEER_TOPK

    def head(h, _):
        base = pl.multiple_of(h * 2 * nk, 2 * nk)
        v0, i0 = _top16(st_ref[pl.ds(base, nk), :])
        v1, i1 = _top16(st_ref[pl.ds(base + nk, nk), :])
        cand = _pair_grid(v0, v1, lambda x, y: x + y, -jnp.inf)
        cidx = _pair_grid(i0, i1, lambda x, y: x * float(nk) + y, 0.0)
        best, eidx = _top16(cand, cidx)
        ex = jnp.exp(best - jnp.max(best, axis=0, keepdims=True))
        gate = ex / jnp.sum(ex, axis=0, keepdims=True)
        row = pl.multiple_of(h * k, k)
        es_ref[pl.ds(row, k), :] = eidx.astype(jnp.int32)
        gs_ref[pl.ds(row, k), :] = gate
        return 0

    lax.fori_loop(0, PEER_HEADS, head, 0)
    e_ref[...] = es_ref[...].T * ROWS_PER_EXPERT
    g_ref[...] = gs_ref[...].T


def _topk(st):
    ns, t = st.shape
    tt = 256
    out = pl.BlockSpec((tt, PEER_SLOTS), lambda i: (i, 0))
    return pl.pallas_call(
        _topk_kernel,
        grid=(t // tt,),
        in_specs=[pl.BlockSpec((ns, tt), lambda i: (0, i))],
        out_specs=[out, out],
        out_shape=[jax.ShapeDtypeStruct((t, PEER_SLOTS), jnp.int32),
                   jax.ShapeDtypeStruct((t, PEER_SLOTS), F32)],
        scratch_shapes=[pltpu.VMEM((PEER_SLOTS, tt), jnp.int32), pltpu.VMEM((PEER_SLOTS, tt), F32)],
        compiler_params=_cparams(("parallel",)),
        name="peer_topk",
    )(st)


PEER_TB = 128
ROWS_PER_EXPERT = 4
TILE_ROWS = PEER_SLOTS * ROWS_PER_EXPERT


def _pack_kernel(x_ref, o_ref):
    bits = pltpu.bitcast(x_ref[...], jnp.uint32)
    rne = bits + jnp.uint32(0x7FFF) + ((bits >> 16) & jnp.uint32(1))
    half = x_ref.shape[1] // 2
    word = (rne[:, :half] >> 16) | (rne[:, half:] & jnp.uint32(0xFFFF0000))
    o_ref[...] = pltpu.bitcast(word, jnp.int32).reshape(o_ref.shape)


def _pack_table(tab):
    e, d = tab.shape
    te = 256
    return pl.pallas_call(
        _pack_kernel,
        grid=(e // te,),
        in_specs=[pl.BlockSpec((te, d), lambda i: (i, 0))],
        out_specs=pl.BlockSpec((te * ROWS_PER_EXPERT, LANES), lambda i: (i, 0)),
        out_shape=jax.ShapeDtypeStruct((e * ROWS_PER_EXPERT, LANES), jnp.int32),
        compiler_params=_cparams(("parallel",)),
        name="pack_table",
    )(tab)


def _peer_layout():
    j = np.arange(2 * TILE_ROWS)
    chunk = (j % 8) // 2 + 4 * (j % 2)
    mask8 = (chunk[None, :] == np.arange(8)[:, None]).astype(np.float32)
    group = (j[:, None] // 8 == np.arange(PEER_SLOTS)[None, :]).astype(np.float32)
    return jnp.asarray(mask8), jnp.asarray(group, BF16), jnp.asarray(group.T, BF16)


def _load_table(tab_hbm, tab_vmem, sem):
    @pl.when(pl.program_id(0) == 0)
    def _():
        cp = pltpu.make_async_copy(tab_hbm, tab_vmem, sem)
        cp.start()
        cp.wait()


PEER_GROUP = 8


def _token_tile(idx_ref, tab_ref, token):
    tok_idx = idx_ref.at[pl.ds(token * PEER_SLOTS, PEER_SLOTS)]
    slabs = [tab_ref[pl.ds(pl.multiple_of(tok_idx[s], ROWS_PER_EXPERT), ROWS_PER_EXPERT), :]
             for s in range(PEER_SLOTS)]
    return pltpu.bitcast(jnp.concatenate(slabs, axis=0), BF16)


def _split_bf16(a):
    hi = a.astype(BF16)
    lo = (a - hi.astype(F32)).astype(BF16)
    return jnp.concatenate([hi, lo], axis=0)


def _dot_hilo(a, b01):
    hi = a.astype(BF16)
    lo = (a - hi.astype(F32)).astype(BF16)
    return jnp.dot(hi, b01, preferred_element_type=F32) + jnp.dot(lo, b01, preferred_element_type=F32)


def _peer_u_kernel(idx_ref, x_ref, mask_ref, tab_hbm, o_ref, tab_ref, sem):
    _load_table(tab_hbm, tab_ref, sem)
    mask8 = mask_ref[...]

    def group(t8, _):
        rows8 = pl.ds(pl.multiple_of(t8 * PEER_GROUP, PEER_GROUP), PEER_GROUP)
        x8 = x_ref[rows8, :].reshape(PEER_GROUP, 8, LANES)
        rows = []
        for j in range(PEER_GROUP):
            tile = _token_tile(idx_ref, tab_ref, t8 * PEER_GROUP + j)
            d = lax.dot_general(_split_bf16(x8[j]), tile, _NT, preferred_element_type=F32)
            rows.append(jnp.sum((d[:8] + d[8:]) * mask8, axis=0, keepdims=True))
        o_ref[rows8, :] = jnp.concatenate(rows, axis=0)
        return 0

    lax.fori_loop(0, PEER_TB // PEER_GROUP, group, 0)


def _peer_v_kernel(idx_ref, a_ref, g_ref, mask_ref, grp_ref, grpt_ref, x1_ref, gt_ref, gfin_ref, tab_hbm, o_ref,
                   tab_ref, wx_ref, sem, *, final):
    _load_table(tab_hbm, tab_ref, sem)
    mask8 = mask_ref[...]
    act = _dot_hilo(a_ref[...], grp_ref[...])
    gelu = 0.5 * act * (1.0 + lax.erf(act * math.sqrt(0.5)))
    wx_ref[...] = _dot_hilo(g_ref[...] * gelu, grpt_ref[...])

    def group(t8, _):
        rows8 = pl.ds(pl.multiple_of(t8 * PEER_GROUP, PEER_GROUP), PEER_GROUP)
        w8 = wx_ref[rows8, :]
        outs = []
        for j in range(PEER_GROUP):
            lhs = _split_bf16(w8[j:j + 1, :] * mask8)
            tile = _token_tile(idx_ref, tab_ref, t8 * PEER_GROUP + j)
            out = jnp.dot(lhs, tile, preferred_element_type=F32)
            outs.append(out[:8] + out[8:])
        o_ref[rows8, :] = jnp.stack(outs, axis=0).reshape(PEER_GROUP, 8 * LANES)
        return 0

    lax.fori_loop(0, PEER_TB // PEER_GROUP, group, 0)
    x = x1_ref[...] + gt_ref[...] * o_ref[...]
    o_ref[...] = _rms(x, gfin_ref[...]) if final else x


def _table_scratch():
    return pltpu.VMEM((PEER_N_KEYS * PEER_N_KEYS * ROWS_PER_EXPERT, LANES), jnp.int32)


def _peer_apply(h2, eidx4, gate, tab_u, tab_v, x1, gt2, g_final, seq, final):
    t, d = h2.shape
    tb = PEER_TB
    mask8, grp, grpt = _peer_layout()
    idx = eidx4.reshape(t * PEER_SLOTS)
    idx_spec = pl.BlockSpec((tb * PEER_SLOTS,), lambda i: (i,), memory_space=pltpu.SMEM)
    full = lambda a: pl.BlockSpec(a.shape, lambda i: (0,) * a.ndim)
    tok_rows = pl.BlockSpec((tb, d), lambda i: (i, 0))
    hbm = pl.BlockSpec(memory_space=pl.ANY)
    act = pl.pallas_call(
        _peer_u_kernel,
        grid=(t // tb,),
        in_specs=[idx_spec, tok_rows, full(mask8), hbm],
        out_specs=pl.BlockSpec((tb, 2 * TILE_ROWS), lambda i: (i, 0)),
        out_shape=jax.ShapeDtypeStruct((t, 2 * TILE_ROWS), F32),
        scratch_shapes=[_table_scratch(), pltpu.SemaphoreType.DMA(())],
        compiler_params=_cparams(("arbitrary",)),
        name="peer_u",
    )(idx, h2, mask8, tab_u)
    tpb = seq // tb
    gfin = g_final.reshape(1, d)
    out = pl.pallas_call(
        functools.partial(_peer_v_kernel, final=final),
        grid=(t // tb,),
        in_specs=[idx_spec, pl.BlockSpec((tb, 2 * TILE_ROWS), lambda i: (i, 0)),
                  pl.BlockSpec((tb, PEER_SLOTS), lambda i: (i, 0)), full(mask8), full(grp), full(grpt),
                  tok_rows, pl.BlockSpec((None, 1, d), lambda i: (i // tpb, 0, 0)), full(gfin), hbm],
        out_specs=tok_rows,
        out_shape=jax.ShapeDtypeStruct((t, d), F32),
        scratch_shapes=[_table_scratch(), pltpu.VMEM((tb, 2 * TILE_ROWS), F32),
                        pltpu.SemaphoreType.DMA(())],
        compiler_params=_cparams(("arbitrary",)),
        name="peer_v",
    )(idx, act, gate, mask8, grp, grpt, x1, gt2, gfin, tab_v)
    return out


def _mla_head_cols(rope_cols, nope_cols):
    pad = lambda n: [-1] * n
    r1 = list(rope_cols[:16]) if rope_cols is not None else pad(16)
    r2 = list(rope_cols[16:]) if rope_cols is not None else pad(16)
    n1 = list(nope_cols[:48]) if nope_cols is not None else pad(48)
    n2 = list(nope_cols[48:]) if nope_cols is not None else pad(16)
    return r1 + n1 + r2 + n2 + pad(32)


def _take_cols(w, cols):
    cols = np.asarray(cols)
    out = jnp.take(w, jnp.asarray(np.maximum(cols, 0)), axis=1)
    return jnp.where(jnp.asarray(cols >= 0)[None, :], out, 0.0)


def _layer_weights(w_in, w_uq, w_ukv):
    qk = MLA_QK_DIM
    uq_cols, uk_cols, uv_cols = [], [], []
    for h in range(MLA_HEADS):
        uq_cols += _mla_head_cols(range(h * qk + MLA_QK_NOPE, (h + 1) * qk), range(h * qk, h * qk + MLA_QK_NOPE))
        kv0 = h * (MLA_QK_NOPE + MLA_V_DIM)
        uk_cols += _mla_head_cols(None, range(kv0, kv0 + MLA_QK_NOPE))
        uv_cols += list(range(kv0 + MLA_QK_NOPE, kv0 + MLA_QK_NOPE + MLA_V_DIM)) + [-1] * (LANES - MLA_V_DIM)
    o_kr = MLA_Q_RANK + MLA_KV_RANK
    o_r = o_kr + MLA_QK_ROPE
    a_cols = list(range(o_kr)) + _mla_head_cols(range(o_kr, o_r), None)

    def pair_cols(base):
        cols = []
        for p in range(RET_HEADS // 2):
            a, b = base + 2 * p * RET_HEAD_DIM, base + (2 * p + 1) * RET_HEAD_DIM
            cols += list(range(a, a + 32)) + list(range(b, b + 32)) + list(range(a + 32, a + 64)) + list(range(b + 32, b + 64))
        return cols

    r_cols = (pair_cols(o_r) + pair_cols(o_r + RET_WIDTH)
              + list(range(o_r + 2 * RET_WIDTH, o_r + 4 * RET_WIDTH)))
    bf = lambda a: a.astype(BF16)
    return (bf(_take_cols(w_in, a_cols)), bf(_take_cols(w_in, r_cols)), bf(_take_cols(w_uq, uq_cols)),
            bf(_take_cols(w_ukv, uk_cols)), bf(_take_cols(w_ukv, uv_cols)))


def kernel(x, c, positions, w_ada, b_ada, g_norm1, w_in, g_q_norm, w_uq, g_kv_norm, w_ukv, g_ret_norm,
           w_out, g_norm2, w_query, sub_keys, expert_u, expert_v, g_final):
    b, s, d = x.shape
    t = b * s
    depth = w_ada.shape[0]
    tabs = _rope_tables(positions)
    x2 = x.reshape(t, d)
    for l in range(depth):
        mod = _adaln(c, w_ada[l], b_ada[l])
        sh1, sc1, gt1, sh2, sc2, gt2 = [m.reshape(b, 1, d) for m in jnp.split(mod, 6, axis=-1)]
        wa, wr, wuq, wuk, wuv = _layer_weights(w_in[l], w_uq[l], w_ukv[l])
        q, k, v, rq, rk, rv, rg = _proj(
            x2, sc1, sh1, g_norm1[l].reshape(1, d), wa, wr, g_q_norm[l].reshape(1, -1), wuq,
            g_kv_norm[l].reshape(1, -1), wuk, wuv, tabs, s)
        r3 = lambda a: a.reshape(b, s, a.shape[-1])
        y_mla = _attention(r3(q), r3(k), r3(v)).reshape(t, -1)
        y_ret = _retention(r3(rq), r3(rk), r3(rv), r3(rg), g_ret_norm[l]).reshape(t, -1)
        x1, h2, st = _mix(y_mla, y_ret, x2, gt1, sc2, sh2, g_norm2[l].reshape(1, d),
                          w_out[l].astype(BF16), w_query[l].astype(BF16), sub_keys[l].astype(BF16), s)
        eidx4, gate = _topk(st)
        x2 = _peer_apply(h2, eidx4, gate, _pack_table(expert_u[l]), _pack_table(expert_v[l]),
                         x1, gt2, g_final, s, final=(l == depth - 1))
    return x2.reshape(b, s, d)
```

```python
import functools
import math

import jax
import jax.numpy as jnp
import numpy as np
from jax import lax
from jax.experimental import pallas as pl
from jax.experimental.pallas import tpu as pltpu

F32 = jnp.float32
BF16 = jnp.bfloat16
HIGHEST = lax.Precision.HIGHEST

EPS = 1e-6
ROPE_THETA = 10000.0

MLA_HEADS = 8
MLA_QK_NOPE = 64
MLA_QK_ROPE = 32
MLA_QK_DIM = MLA_QK_NOPE + MLA_QK_ROPE
MLA_V_DIM = 64
MLA_Q_RANK = 256
MLA_KV_RANK = 128

RET_HEADS = 8
RET_HEAD_DIM = 64
RET_WIDTH = RET_HEADS * RET_HEAD_DIM
RET_CHUNK = 128

PEER_HEADS = 8
PEER_N_KEYS = 128
PEER_HALF = 128
PEER_TOPK = 16
PEER_SLOTS = PEER_HEADS * PEER_TOPK

LANES = 128
VMEM_LIMIT = 56 * 1024 * 1024

_NT = (((1,), (1,)), ((), ()))


def _cparams(sem):
    return pltpu.CompilerParams(dimension_semantics=sem, vmem_limit_bytes=VMEM_LIMIT)


def _adaln_kernel(c_ref, w_ref, b_ref, o_ref):
    c = c_ref[...]
    s = c * jax.nn.sigmoid(c)
    o_ref[...] = jnp.dot(s, w_ref[...], precision=HIGHEST, preferred_element_type=F32) + b_ref[...]


def _adaln(c, w, b):
    bsz, d = c.shape
    n = w.shape[1]
    return pl.pallas_call(
        _adaln_kernel,
        grid=(n // d,),
        in_specs=[
            pl.BlockSpec((bsz, d), lambda j: (0, 0)),
            pl.BlockSpec((d, d), lambda j: (0, j)),
            pl.BlockSpec((1, d), lambda j: (0, j)),
        ],
        out_specs=pl.BlockSpec((bsz, d), lambda j: (0, j)),
        out_shape=jax.ShapeDtypeStruct((bsz, n), F32),
        compiler_params=_cparams(("arbitrary",)),
        name="adaln",
    )(c, w, b.reshape(1, n))


def _rope_tab_kernel(pos_ref, inv_ref, sg_ref, cr_ref, sr_ref, cm_ref, sm_ref):
    ang = pos_ref[...] * inv_ref[...]
    cos, sin = jnp.cos(ang), jnp.sin(ang)
    nr, nm = RET_HEAD_DIM // 2, MLA_QK_ROPE // 2
    ret = lambda a: jnp.concatenate([a[:, :nr]] * (LANES // nr), axis=1)
    mla = lambda a, fill: jnp.concatenate(
        [a[:, nr:nr + nm], jnp.full((a.shape[0], 64 - nm), fill, F32)] * 2, axis=1)
    cr_ref[...] = ret(cos)
    sr_ref[...] = ret(sin) * sg_ref[0:1, :]
    cm_ref[...] = mla(cos, 1.0)
    sm_ref[...] = mla(sin, 0.0) * sg_ref[1:2, :]


def _rope_tables(positions):
    t = positions.size
    pos = positions.reshape(t, 1).astype(F32)
    inv_r = ROPE_THETA ** (-jnp.arange(0, RET_HEAD_DIM, 2, dtype=F32) / RET_HEAD_DIM)
    inv_m = ROPE_THETA ** (-jnp.arange(0, MLA_QK_ROPE, 2, dtype=F32) / MLA_QK_ROPE)
    z = lambda n: jnp.zeros((n,), F32)
    o = lambda n: jnp.ones((n,), F32)
    inv = jnp.concatenate([inv_r, inv_m, z(LANES - 48)]).reshape(1, LANES)
    sg = jnp.stack([
        jnp.concatenate([-o(64), o(64)]),
        jnp.concatenate([-o(16), z(48), o(16), z(48)]),
    ])
    tm = 512
    tab = jax.ShapeDtypeStruct((t, LANES), F32)
    spec = pl.BlockSpec((tm, LANES), lambda i: (i, 0))
    return pl.pallas_call(
        _rope_tab_kernel,
        grid=(t // tm,),
        in_specs=[pl.BlockSpec((tm, 1), lambda i: (i, 0)), pl.BlockSpec((1, LANES), lambda i: (0, 0)),
                  pl.BlockSpec((2, LANES), lambda i: (0, 0))],
        out_specs=[spec] * 4,
        out_shape=[tab] * 4,
        compiler_params=_cparams(("parallel",)),
        name="rope_tables",
    )(pos, inv, sg)


def _rms(x, g):
    return x * lax.rsqrt(jnp.mean(x * x, axis=-1, keepdims=True) + EPS) * g


def _rot(x, c, s):
    return x * c + pltpu.roll(x, 64, 1) * s


def _proj_kernel(x_ref, sc_ref, sh_ref, g1_ref, wa_ref, wr_ref, gq_ref, wuq_ref, gkv_ref,
                 wuk_ref, wuv_ref, cr_ref, sr_ref, cm_ref, sm_ref,
                 q_ref, k_ref, v_ref, rq_ref, rk_ref, rv_ref, rg_ref):
    x = x_ref[...]
    h = _rms(x, g1_ref[...]) * (1.0 + sc_ref[...]) + sh_ref[...]
    hb = h.astype(BF16)
    cm, sm = cm_ref[...], sm_ref[...]
    cr, sr = cr_ref[...], sr_ref[...]

    pa = jnp.dot(hb, wa_ref[...], preferred_element_type=F32)
    q_lat = pa[:, :MLA_Q_RANK]
    kv_lat = pa[:, MLA_Q_RANK:MLA_Q_RANK + MLA_KV_RANK]
    kr = _rot(pa[:, MLA_Q_RANK + MLA_KV_RANK:], cm, sm)

    qn = _rms(q_lat, gq_ref[...]).astype(BF16)
    q = jnp.dot(qn, wuq_ref[...], preferred_element_type=F32)
    scale = MLA_QK_DIM ** -0.5 * math.log2(math.e)
    for hd in range(MLA_HEADS):
        sl = slice(hd * LANES, (hd + 1) * LANES)
        q_ref[:, sl] = (_rot(q[:, sl], cm, sm) * scale).astype(BF16)

    kvn = _rms(kv_lat, gkv_ref[...]).astype(BF16)
    k = jnp.dot(kvn, wuk_ref[...], preferred_element_type=F32)
    for hd in range(MLA_HEADS):
        sl = slice(hd * LANES, (hd + 1) * LANES)
        k_ref[:, sl] = (k[:, sl] + kr).astype(BF16)
    v = jnp.dot(kvn, wuv_ref[...], preferred_element_type=F32)
    vlane = lax.broadcasted_iota(jnp.int32, v.shape, 1) & (LANES - 1)
    v_ref[...] = jnp.where(vlane == MLA_V_DIM, 1.0, v).astype(BF16)

    pr = jnp.dot(hb, wr_ref[...], preferred_element_type=F32)
    w = RET_WIDTH
    for p in range(RET_HEADS // 2):
        sl = slice(p * LANES, (p + 1) * LANES)
        rq_ref[:, sl] = _rot(pr[:, p * LANES:(p + 1) * LANES], cr, sr).astype(BF16)
        rk_ref[:, sl] = (_rot(pr[:, w + p * LANES:w + (p + 1) * LANES], cr, sr)
                         * (RET_HEAD_DIM ** -0.5)).astype(BF16)
    rv_ref[...] = pr[:, 2 * w:3 * w].astype(BF16)
    rg_ref[...] = pr[:, 3 * w:]


def _proj(x2, sc1, sh1, g1, wa, wr, gq, wuq, gkv, wuk, wuv, tabs, seq):
    t, d = x2.shape
    tm = 512
    tpb = seq // tm
    cr, sr, cm, sm = tabs
    row = lambda n: pl.BlockSpec((tm, n), lambda i: (i, 0))
    full = lambda a: pl.BlockSpec(a.shape, lambda i: (0,) * a.ndim)
    mod = pl.BlockSpec((None, 1, d), lambda i: (i // tpb, 0, 0))
    outs = [(8 * LANES, BF16), (8 * LANES, BF16), (8 * LANES, BF16), (512, BF16), (512, BF16),
            (512, BF16), (512, F32)]
    return pl.pallas_call(
        _proj_kernel,
        grid=(t // tm,),
        in_specs=[row(d), mod, mod, full(g1), full(wa), full(wr), full(gq), full(wuq), full(gkv),
                  full(wuk), full(wuv), row(LANES), row(LANES), row(LANES), row(LANES)],
        out_specs=[row(n) for n, _ in outs],
        out_shape=[jax.ShapeDtypeStruct((t, n), dt) for n, dt in outs],
        compiler_params=_cparams(("parallel",)),
        name="in_proj",
    )(x2, sc1, sh1, g1, wa, wr, gq, wuq, gkv, wuk, wuv, cr, sr, cm, sm)


def _softmax_update(q, kb, vb, m, acc, mask=None):
    s = lax.dot_general(q, kb, _NT, preferred_element_type=F32)
    if mask is not None:
        s = jnp.where(mask, s, -1e30)
    m_new = jnp.maximum(m, jnp.max(s, axis=1, keepdims=True))
    p = jnp.exp2(s - m_new)
    acc = jnp.exp2(m - m_new) * acc + jnp.dot(p.astype(BF16), vb, preferred_element_type=F32)
    return m_new, acc


def _attn_kernel(q_ref, k_ref, v_ref, o_ref, *, tq):
    i = pl.program_id(2)
    half = tq // 2
    heads = [slice(hh * LANES, (hh + 1) * LANES) for hh in range(2)]
    qs = [q_ref[:, hs] for hs in heads]

    def full_block(j, carry):
        off = pl.multiple_of(j * tq, tq)
        return tuple(_softmax_update(q, k_ref[pl.ds(off, tq), hs], v_ref[pl.ds(off, tq), hs], m, acc)
                     for hs, q, (m, acc) in zip(heads, qs, carry))

    carry = ((jnp.full((tq, 1), -1e30, F32), jnp.zeros((tq, LANES), F32)),) * 2
    carry = lax.fori_loop(0, i, full_block, carry)

    off = pl.multiple_of(i * tq, tq)
    tri = (lax.broadcasted_iota(jnp.int32, (half, half), 1) <= lax.broadcasted_iota(jnp.int32, (half, half), 0))
    low = (lax.broadcasted_iota(jnp.int32, (half, tq), 1) <= lax.broadcasted_iota(jnp.int32, (half, tq), 0) + half)
    outs = []
    for hs, q, (m, acc) in zip(heads, qs, carry):
        m0, a0 = _softmax_update(q[:half], k_ref[pl.ds(off, half), hs], v_ref[pl.ds(off, half), hs],
                                 m[:half], acc[:half], tri)
        m1, a1 = _softmax_update(q[half:], k_ref[pl.ds(off, tq), hs], v_ref[pl.ds(off, tq), hs],
                                 m[half:], acc[half:], low)
        acc = jnp.concatenate([a0, a1], axis=0)
        outs.append(acc[:, :MLA_V_DIM] / acc[:, MLA_V_DIM:MLA_V_DIM + 1])
    o_ref[...] = jnp.concatenate(outs, axis=1).astype(o_ref.dtype)


def _attention(q, k, v):
    b, s, _ = q.shape
    tq = 1024
    return pl.pallas_call(
        functools.partial(_attn_kernel, tq=tq),
        grid=(b, MLA_HEADS // 2, s // tq),
        in_specs=[
            pl.BlockSpec((None, tq, 2 * LANES), lambda bi, p, i: (bi, i, p)),
            pl.BlockSpec((None, s, 2 * LANES), lambda bi, p, i: (bi, 0, p)),
            pl.BlockSpec((None, s, 2 * LANES), lambda bi, p, i: (bi, 0, p)),
        ],
        out_specs=pl.BlockSpec((None, tq, LANES), lambda bi, p, i: (bi, i, p)),
        out_shape=jax.ShapeDtypeStruct((b, s, MLA_HEADS * MLA_V_DIM), BF16),
        compiler_params=_cparams(("parallel", "parallel", "arbitrary")),
        name="mla_attention",
    )(q, k, v)


def _ret_kernel(q_ref, k_ref, v_ref, g_ref, dm_ref, z_ref, xi_ref, dec_ref, gn_ref, o_ref, st_ref,
                *, nchunk):
    c = RET_CHUNK

    @pl.when(pl.program_id(1) == 0)
    def _():
        st_ref[...] = jnp.zeros_like(st_ref)

    lane = lax.broadcasted_iota(jnp.int32, (c, LANES), 1)
    sub = lax.broadcasted_iota(jnp.int32, (c, LANES), 0)
    v_first = lane < RET_HEAD_DIM
    k_first = (lane & 32) == 0
    same_head = ((sub & 32) == 0) == v_first

    def head_mean(a):
        tot = jnp.sum(a, axis=1, keepdims=True)
        first = jnp.sum(jnp.where(v_first, a, 0.0), axis=1, keepdims=True)
        return jnp.where(v_first, first, tot - first) * (1.0 / RET_HEAD_DIM)

    for ci in range(nchunk):
        sl = slice(ci * c, (ci + 1) * c)
        for p in range(RET_HEADS // 2):
            ps = slice(p * LANES, (p + 1) * LANES)
            q, k, v = q_ref[sl, ps], k_ref[sl, ps], v_ref[sl, ps]
            zero = jnp.zeros_like(q)
            s_a = lax.dot_general(jnp.where(k_first, q, zero), k, _NT, preferred_element_type=F32) * dm_ref[2 * p]
            s_b = lax.dot_general(jnp.where(k_first, zero, q), k, _NT, preferred_element_type=F32) * dm_ref[2 * p + 1]
            s_ab = jnp.concatenate([s_a, s_b], axis=1).astype(BF16)
            v_bd = jnp.concatenate([jnp.where(v_first, v, zero), jnp.where(v_first, zero, v)], axis=0)
            y = jnp.dot(s_ab, v_bd, preferred_element_type=F32)

            st = st_ref[p]
            st_hi = st.astype(BF16)
            st_lo = (st - st_hi.astype(F32)).astype(BF16)
            y = y + (jnp.dot(q, st_hi, preferred_element_type=F32)
                     + jnp.dot(q, st_lo, preferred_element_type=F32)) * xi_ref[p]

            vz = (v.astype(F32) * z_ref[p]).astype(BF16)
            kt = k.astype(F32).T.astype(BF16)
            kv = jnp.dot(kt, vz, preferred_element_type=F32)
            st_ref[p] = st * dec_ref[p] + jnp.where(same_head, kv, 0.0)

            dlt = y - head_mean(y)
            yn = dlt * lax.rsqrt(head_mean(dlt * dlt) + EPS) * gn_ref[:, ps]
            gate = g_ref[sl, ps]
            o_ref[sl, ps] = (gate * jax.nn.sigmoid(gate) * yn).astype(o_ref.dtype)


def _retention_consts():
    h, c = RET_HEADS, RET_CHUNK
    gamma = 1.0 - 2.0 ** (-5.0 - jnp.arange(h, dtype=F32))
    log_g = jnp.log(gamma)
    idx = jnp.arange(c, dtype=F32)
    diff = idx[:, None] - idx[None, :]
    dmask = jnp.where(diff >= 0, jnp.exp(log_g[:, None, None] * jnp.maximum(diff, 0.0)), 0.0)
    zeta = jnp.exp(log_g[:, None] * (c - 1.0 - idx))
    xi = jnp.exp(log_g[:, None] * (idx + 1.0))
    decay = jnp.exp(log_g * c)
    by_lane = lambda a: jnp.repeat(a.reshape(h // 2, 2, -1), RET_HEAD_DIM, axis=1)
    z = by_lane(zeta).transpose(0, 2, 1)
    x = by_lane(xi).transpose(0, 2, 1)
    dec = by_lane(decay[:, None]).transpose(0, 2, 1)
    return dmask, z, x, dec


def _retention(rq, rk, rv, rg, g_ret):
    b, s, w = rq.shape
    tc = 512
    consts = _retention_consts()
    blk = pl.BlockSpec((None, tc, w), lambda bi, t: (bi, t, 0))
    full = lambda a: pl.BlockSpec(a.shape, lambda bi, t: (0,) * a.ndim)
    gn = g_ret.reshape(1, w)
    return pl.pallas_call(
        functools.partial(_ret_kernel, nchunk=tc // RET_CHUNK),
        grid=(b, s // tc),
        in_specs=[blk, blk, blk, blk] + [full(a) for a in consts] + [full(gn)],
        out_specs=blk,
        out_shape=jax.ShapeDtypeStruct((b, s, w), BF16),
        scratch_shapes=[pltpu.VMEM((RET_HEADS // 2, LANES, LANES), F32)],
        compiler_params=_cparams(("parallel", "arbitrary")),
        name="retention",
    )(rq, rk, rv, rg, *consts, gn)


def _mix_kernel(ym_ref, yr_ref, x_ref, gt_ref, sc_ref, sh_ref, g2_ref, wo_ref, wq_ref, keys_ref,
                x1_ref, h2_ref, st_ref):
    half = ym_ref.shape[1]
    mixed = (jnp.dot(ym_ref[...], wo_ref[:half, :], preferred_element_type=F32)
             + jnp.dot(yr_ref[...], wo_ref[half:, :], preferred_element_type=F32))
    x1 = x_ref[...] + gt_ref[...] * mixed
    x1_ref[...] = x1
    h2 = _rms(x1, g2_ref[...]) * (1.0 + sc_ref[...]) + sh_ref[...]
    h2_ref[...] = h2
    pq = jnp.dot(h2.astype(BF16), wq_ref[...], preferred_element_type=F32)
    for g in range(2 * PEER_HEADS):
        qg = pq[:, g * PEER_HALF:(g + 1) * PEER_HALF].astype(BF16)
        st_ref[g * PEER_N_KEYS:(g + 1) * PEER_N_KEYS, :] = lax.dot_general(
            keys_ref[g % 2], qg, _NT, preferred_element_type=F32)


def _mix(ym, yr, x2, gt1, sc2, sh2, g2, wo, wq, keys, seq):
    t, d = x2.shape
    tm = 512
    tpb = seq // tm
    row = lambda n: pl.BlockSpec((tm, n), lambda i: (i, 0))
    full = lambda a: pl.BlockSpec(a.shape, lambda i: (0,) * a.ndim)
    mod = pl.BlockSpec((None, 1, d), lambda i: (i // tpb, 0, 0))
    ns = 2 * PEER_HEADS * PEER_N_KEYS
    return pl.pallas_call(
        _mix_kernel,
        grid=(t // tm,),
        in_specs=[row(ym.shape[1]), row(yr.shape[1]), row(d), mod, mod, mod, full(g2), full(wo),
                  full(wq), full(keys)],
        out_specs=[row(d), row(d), pl.BlockSpec((ns, tm), lambda i: (0, i))],
        out_shape=[jax.ShapeDtypeStruct((t, d), F32), jax.ShapeDtypeStruct((t, d), F32),
                   jax.ShapeDtypeStruct((ns, t), F32)],
        compiler_params=_cparams(("parallel",)),
        name="out_proj_peer_scores",
    )(ym, yr, x2, gt1, sc2, sh2, g2, wo, wq, keys)


def _top16(s, payload=None):
    rows = lax.broadcasted_iota(jnp.int32, s.shape, 0).astype(F32)
    vals, sel = [], []
    for _ in range(PEER_TOPK):
        m = jnp.max(s, axis=0, keepdims=True)
        at = jnp.min(jnp.where(s == m, rows, float(s.shape[0])), axis=0, keepdims=True)
        hit = rows == at
        vals.append(m)
        sel.append(at if payload is None else jnp.max(jnp.where(hit, payload, -1.0), axis=0, keepdims=True))
        s = jnp.where(hit, -jnp.inf, s)
    return jnp.concatenate(vals, axis=0), jnp.concatenate(sel, axis=0)


def _pair_grid(r0, r1, combine, fill):
    k = PEER_TOPK
    sub = lax.broadcasted_iota(jnp.int32, (8, r0.shape[1]), 0)
    parts = [combine(r0[0:1], r1), combine(r0[1:2], r1[0:8])]
    for a in range(2, 8):
        parts.append(jnp.where(sub < k // (a + 1), combine(r0[a:a + 1], r1[0:8]), fill))
    parts.append(combine(r0[8:16], r1[0:1]))
    return jnp.concatenate(parts, axis=0)


def _topk_kernel(st_ref, e_ref, g_ref, es_ref, gs_ref):
    nk, k = PEER_N_KEYS, PEER_TOPK

    def head(h, _):
        base = pl.multiple_of(h * 2 * nk, 2 * nk)
        v0, i0 = _top16(st_ref[pl.ds(base, nk), :])
        v1, i1 = _top16(st_ref[pl.ds(base + nk, nk), :])
        cand = _pair_grid(v0, v1, lambda x, y: x + y, -jnp.inf)
        cidx = _pair_grid(i0, i1, lambda x, y: x * float(nk) + y, 0.0)
        best, eidx = _top16(cand, cidx)
        ex = jnp.exp(best - jnp.max(best, axis=0, keepdims=True))
        gate = ex / jnp.sum(ex, axis=0, keepdims=True)
        row = pl.multiple_of(h * k, k)
        es_ref[pl.ds(row, k), :] = eidx.astype(jnp.int32)
        gs_ref[pl.ds(row, k), :] = gate
        return 0

    lax.fori_loop(0, PEER_HEADS, head, 0)
    e_ref[...] = es_ref[...].T * ROWS_PER_EXPERT
    g_ref[...] = gs_ref[...].T


def _topk(st):
    ns, t = st.shape
    tt = 512
    out = pl.BlockSpec((tt, PEER_SLOTS), lambda i: (i, 0))
    return pl.pallas_call(
        _topk_kernel,
        grid=(t // tt,),
        in_specs=[pl.BlockSpec((ns, tt), lambda i: (0, i))],
        out_specs=[out, out],
        out_shape=[jax.ShapeDtypeStruct((t, PEER_SLOTS), jnp.int32),
                   jax.ShapeDtypeStruct((t, PEER_SLOTS), F32)],
        scratch_shapes=[pltpu.VMEM((PEER_SLOTS, tt), jnp.int32), pltpu.VMEM((PEER_SLOTS, tt), F32)],
        compiler_params=_cparams(("parallel",)),
        name="peer_topk",
    )(st)


PEER_TB = 128
ROWS_PER_EXPERT = 4
TILE_ROWS = PEER_SLOTS * ROWS_PER_EXPERT


def _pack_kernel(x_ref, o_ref):
    bits = pltpu.bitcast(x_ref[...], jnp.uint32)
    rne = bits + jnp.uint32(0x7FFF) + ((bits >> 16) & jnp.uint32(1))
    half = x_ref.shape[1] // 2
    word = (rne[:, :half] >> 16) | (rne[:, half:] & jnp.uint32(0xFFFF0000))
    o_ref[...] = pltpu.bitcast(word, jnp.int32).reshape(o_ref.shape)


def _pack_table(tab):
    e, d = tab.shape
    te = 256
    return pl.pallas_call(
        _pack_kernel,
        grid=(e // te,),
        in_specs=[pl.BlockSpec((te, d), lambda i: (i, 0))],
        out_specs=pl.BlockSpec((te * ROWS_PER_EXPERT, LANES), lambda i: (i, 0)),
        out_shape=jax.ShapeDtypeStruct((e * ROWS_PER_EXPERT, LANES), jnp.int32),
        compiler_params=_cparams(("parallel",)),
        name="pack_table",
    )(tab)


def _peer_layout():
    j = np.arange(2 * TILE_ROWS)
    chunk = (j % 8) // 2 + 4 * (j % 2)
    mask8 = (chunk[None, :] == np.arange(8)[:, None]).astype(np.float32)
    group = (j[:, None] // 8 == np.arange(PEER_SLOTS)[None, :]).astype(np.float32)
    return jnp.asarray(mask8), jnp.asarray(group, BF16), jnp.asarray(group.T, BF16)


def _load_table(tab_hbm, tab_vmem, sem):
    @pl.when(pl.program_id(0) == 0)
    def _():
        cp = pltpu.make_async_copy(tab_hbm, tab_vmem, sem)
        cp.start()
        cp.wait()


PEER_GROUP = 8
PEER_GROUPS_PER_STEP = 4


def _for_groups(group):
    def step(i, _):
        for u in range(PEER_GROUPS_PER_STEP):
            group(i * PEER_GROUPS_PER_STEP + u)
        return 0

    lax.fori_loop(0, PEER_TB // (PEER_GROUP * PEER_GROUPS_PER_STEP), step, 0)


def _token_tile(idx_ref, tab_ref, token):
    tok_idx = idx_ref.at[pl.ds(token * PEER_SLOTS, PEER_SLOTS)]
    slabs = [tab_ref[pl.ds(pl.multiple_of(tok_idx[s], ROWS_PER_EXPERT), ROWS_PER_EXPERT), :]
             for s in range(PEER_SLOTS)]
    return pltpu.bitcast(jnp.concatenate(slabs, axis=0), BF16)


def _split_bf16(a):
    hi = a.astype(BF16)
    lo = (a - hi.astype(F32)).astype(BF16)
    return jnp.concatenate([hi, lo], axis=0)


def _dot_hilo(a, b01):
    hi = a.astype(BF16)
    lo = (a - hi.astype(F32)).astype(BF16)
    return jnp.dot(hi, b01, preferred_element_type=F32) + jnp.dot(lo, b01, preferred_element_type=F32)


def _peer_u_kernel(idx_ref, x_ref, mask_ref, tab_hbm, o_ref, tab_ref, sem):
    _load_table(tab_hbm, tab_ref, sem)
    mask8 = mask_ref[...]

    def group(t8):
        rows8 = pl.ds(pl.multiple_of(t8 * PEER_GROUP, PEER_GROUP), PEER_GROUP)
        x8 = x_ref[rows8, :].reshape(PEER_GROUP, 8, LANES)
        rows = []
        for j in range(PEER_GROUP):
            tile = _token_tile(idx_ref, tab_ref, t8 * PEER_GROUP + j)
            d = lax.dot_general(_split_bf16(x8[j]), tile, _NT, preferred_element_type=F32)
            rows.append(jnp.sum((d[:8] + d[8:]) * mask8, axis=0, keepdims=True))
        o_ref[rows8, :] = jnp.concatenate(rows, axis=0)

    _for_groups(group)


def _peer_v_kernel(idx_ref, a_ref, g_ref, mask_ref, grp_ref, grpt_ref, x1_ref, gt_ref, gfin_ref, tab_hbm, o_ref,
                   tab_ref, wx_ref, sem, *, final):
    _load_table(tab_hbm, tab_ref, sem)
    mask8 = mask_ref[...]
    act = _dot_hilo(a_ref[...], grp_ref[...])
    gelu = 0.5 * act * (1.0 + lax.erf(act * math.sqrt(0.5)))
    wx_ref[...] = _dot_hilo(g_ref[...] * gelu, grpt_ref[...])

    def group(t8):
        rows8 = pl.ds(pl.multiple_of(t8 * PEER_GROUP, PEER_GROUP), PEER_GROUP)
        w8 = wx_ref[rows8, :]
        outs = []
        for j in range(PEER_GROUP):
            lhs = _split_bf16(w8[j:j + 1, :] * mask8)
            tile = _token_tile(idx_ref, tab_ref, t8 * PEER_GROUP + j)
            out = jnp.dot(lhs, tile, preferred_element_type=F32)
            outs.append(out[:8] + out[8:])
        o_ref[rows8, :] = jnp.stack(outs, axis=0).reshape(PEER_GROUP, 8 * LANES)

    _for_groups(group)
    x = x1_ref[...] + gt_ref[...] * o_ref[...]
    o_ref[...] = _rms(x, gfin_ref[...]) if final else x


def _table_scratch():
    return pltpu.VMEM((PEER_N_KEYS * PEER_N_KEYS * ROWS_PER_EXPERT, LANES), jnp.int32)


def _peer_apply(h2, eidx4, gate, tab_u, tab_v, x1, gt2, g_final, seq, final):
    t, d = h2.shape
    tb = PEER_TB
    mask8, grp, grpt = _peer_layout()
    idx = eidx4.reshape(t * PEER_SLOTS)
    idx_spec = pl.BlockSpec((tb * PEER_SLOTS,), lambda i: (i,), memory_space=pltpu.SMEM)
    full = lambda a: pl.BlockSpec(a.shape, lambda i: (0,) * a.ndim)
    tok_rows = pl.BlockSpec((tb, d), lambda i: (i, 0))
    hbm = pl.BlockSpec(memory_space=pl.ANY)
    act = pl.pallas_call(
        _peer_u_kernel,
        grid=(t // tb,),
        in_specs=[idx_spec, tok_rows, full(mask8), hbm],
        out_specs=pl.BlockSpec((tb, 2 * TILE_ROWS), lambda i: (i, 0)),
        out_shape=jax.ShapeDtypeStruct((t, 2 * TILE_ROWS), F32),
        scratch_shapes=[_table_scratch(), pltpu.SemaphoreType.DMA(())],
        compiler_params=_cparams(("arbitrary",)),
        name="peer_u",
    )(idx, h2, mask8, tab_u)
    tpb = seq // tb
    gfin = g_final.reshape(1, d)
    out = pl.pallas_call(
        functools.partial(_peer_v_kernel, final=final),
        grid=(t // tb,),
        in_specs=[idx_spec, pl.BlockSpec((tb, 2 * TILE_ROWS), lambda i: (i, 0)),
                  pl.BlockSpec((tb, PEER_SLOTS), lambda i: (i, 0)), full(mask8), full(grp), full(grpt),
                  tok_rows, pl.BlockSpec((None, 1, d), lambda i: (i // tpb, 0, 0)), full(gfin), hbm],
        out_specs=tok_rows,
        out_shape=jax.ShapeDtypeStruct((t, d), F32),
        scratch_shapes=[_table_scratch(), pltpu.VMEM((tb, 2 * TILE_ROWS), F32),
                        pltpu.SemaphoreType.DMA(())],
        compiler_params=_cparams(("arbitrary",)),
        name="peer_v",
    )(idx, act, gate, mask8, grp, grpt, x1, gt2, gfin, tab_v)
    return out


def _mla_head_cols(rope_cols, nope_cols):
    pad = lambda n: [-1] * n
    r1 = list(rope_cols[:16]) if rope_cols is not None else pad(16)
    r2 = list(rope_cols[16:]) if rope_cols is not None else pad(16)
    n1 = list(nope_cols[:48]) if nope_cols is not None else pad(48)
    n2 = list(nope_cols[48:]) if nope_cols is not None else pad(16)
    return r1 + n1 + r2 + n2 + pad(32)


def _take_cols(w, cols):
    cols = np.asarray(cols)
    out = jnp.take(w, jnp.asarray(np.maximum(cols, 0)), axis=1)
    return jnp.where(jnp.asarray(cols >= 0)[None, :], out, 0.0)


def _layer_weights(w_in, w_uq, w_ukv):
    qk = MLA_QK_DIM
    uq_cols, uk_cols, uv_cols = [], [], []
    for h in range(MLA_HEADS):
        uq_cols += _mla_head_cols(range(h * qk + MLA_QK_NOPE, (h + 1) * qk), range(h * qk, h * qk + MLA_QK_NOPE))
        kv0 = h * (MLA_QK_NOPE + MLA_V_DIM)
        uk_cols += _mla_head_cols(None, range(kv0, kv0 + MLA_QK_NOPE))
        uv_cols += list(range(kv0 + MLA_QK_NOPE, kv0 + MLA_QK_NOPE + MLA_V_DIM)) + [-1] * (LANES - MLA_V_DIM)
    o_kr = MLA_Q_RANK + MLA_KV_RANK
    o_r = o_kr + MLA_QK_ROPE
    a_cols = list(range(o_kr)) + _mla_head_cols(range(o_kr, o_r), None)

    def pair_cols(base):
        cols = []
        for p in range(RET_HEADS // 2):
            a, b = base + 2 * p * RET_HEAD_DIM, base + (2 * p + 1) * RET_HEAD_DIM
            cols += list(range(a, a + 32)) + list(range(b, b + 32)) + list(range(a + 32, a + 64)) + list(range(b + 32, b + 64))
        return cols

    r_cols = (pair_cols(o_r) + pair_cols(o_r + RET_WIDTH)
              + list(range(o_r + 2 * RET_WIDTH, o_r + 4 * RET_WIDTH)))
    bf = lambda a: a.astype(BF16)
    return (bf(_take_cols(w_in, a_cols)), bf(_take_cols(w_in, r_cols)), bf(_take_cols(w_uq, uq_cols)),
            bf(_take_cols(w_ukv, uk_cols)), bf(_take_cols(w_ukv, uv_cols)))


def kernel(x, c, positions, w_ada, b_ada, g_norm1, w_in, g_q_norm, w_uq, g_kv_norm, w_ukv, g_ret_norm,
           w_out, g_norm2, w_query, sub_keys, expert_u, expert_v, g_final):
    b, s, d = x.shape
    t = b * s
    depth = w_ada.shape[0]
    tabs = _rope_tables(positions)
    x2 = x.reshape(t, d)
    for l in range(depth):
        mod = _adaln(c, w_ada[l], b_ada[l])
        sh1, sc1, gt1, sh2, sc2, gt2 = [m.reshape(b, 1, d) for m in jnp.split(mod, 6, axis=-1)]
        wa, wr, wuq, wuk, wuv = _layer_weights(w_in[l], w_uq[l], w_ukv[l])
        q, k, v, rq, rk, rv, rg = _proj(
            x2, sc1, sh1, g_norm1[l].reshape(1, d), wa, wr, g_q_norm[l].reshape(1, -1), wuq,
            g_kv_norm[l].reshape(1, -1), wuk, wuv, tabs, s)
        r3 = lambda a: a.reshape(b, s, a.shape[-1])
        y_mla = _attention(r3(q), r3(k), r3(v)).reshape(t, -1)
        y_ret = _retention(r3(rq), r3(rk), r3(rv), r3(rg), g_ret_norm[l]).reshape(t, -1)
        x1, h2, st = _mix(y_mla, y_ret, x2, gt1, sc2, sh2, g_norm2[l].reshape(1, d),
                          w_out[l].astype(BF16), w_query[l].astype(BF16), sub_keys[l].astype(BF16), s)
        eidx4, gate = _topk(st)
        x2 = _peer_apply(h2, eidx4, gate, _pack_table(expert_u[l]), _pack_table(expert_v[l]),
                         x1, gt2, g_final, s, final=(l == depth - 1))
    return x2.reshape(b, s, d)
```

```python
import functools
import math

import jax
import jax.numpy as jnp
import numpy as np
from jax import lax
from jax.experimental import pallas as pl
from jax.experimental.pallas import tpu as pltpu

F32 = jnp.float32
BF16 = jnp.bfloat16
HIGHEST = lax.Precision.HIGHEST

EPS = 1e-6
ROPE_THETA = 10000.0

MLA_HEADS = 8
MLA_QK_NOPE = 64
MLA_QK_ROPE = 32
MLA_QK_DIM = MLA_QK_NOPE + MLA_QK_ROPE
MLA_V_DIM = 64
MLA_Q_RANK = 256
MLA_KV_RANK = 128

RET_HEADS = 8
RET_HEAD_DIM = 64
RET_WIDTH = RET_HEADS * RET_HEAD_DIM
RET_CHUNK = 128

PEER_HEADS = 8
PEER_N_KEYS = 128
PEER_HALF = 128
PEER_TOPK = 16
PEER_SLOTS = PEER_HEADS * PEER_TOPK

LANES = 128
VMEM_LIMIT = 56 * 1024 * 1024

_NT = (((1,), (1,)), ((), ()))


def _cparams(sem):
    return pltpu.CompilerParams(dimension_semantics=sem, vmem_limit_bytes=VMEM_LIMIT)


def _adaln_kernel(c_ref, w_ref, b_ref, o_ref):
    c = c_ref[...]
    s = c * jax.nn.sigmoid(c)
    o_ref[...] = jnp.dot(s, w_ref[...], precision=HIGHEST, preferred_element_type=F32) + b_ref[...]


def _adaln(c, w, b):
    bsz, d = c.shape
    n = w.shape[1]
    return pl.pallas_call(
        _adaln_kernel,
        grid=(n // d,),
        in_specs=[
            pl.BlockSpec((bsz, d), lambda j: (0, 0)),
            pl.BlockSpec((d, d), lambda j: (0, j)),
            pl.BlockSpec((1, d), lambda j: (0, j)),
        ],
        out_specs=pl.BlockSpec((bsz, d), lambda j: (0, j)),
        out_shape=jax.ShapeDtypeStruct((bsz, n), F32),
        compiler_params=_cparams(("arbitrary",)),
        name="adaln",
    )(c, w, b.reshape(1, n))


def _rope_tab_kernel(pos_ref, inv_ref, sg_ref, cr_ref, sr_ref, cm_ref, sm_ref):
    ang = pos_ref[...] * inv_ref[...]
    cos, sin = jnp.cos(ang), jnp.sin(ang)
    nr, nm = RET_HEAD_DIM // 2, MLA_QK_ROPE // 2
    ret = lambda a: jnp.concatenate([a[:, :nr]] * (LANES // nr), axis=1)
    mla = lambda a, fill: jnp.concatenate(
        [a[:, nr:nr + nm], jnp.full((a.shape[0], 64 - nm), fill, F32)] * 2, axis=1)
    cr_ref[...] = ret(cos)
    sr_ref[...] = ret(sin) * sg_ref[0:1, :]
    cm_ref[...] = mla(cos, 1.0)
    sm_ref[...] = mla(sin, 0.0) * sg_ref[1:2, :]


def _rope_tables(positions):
    t = positions.size
    pos = positions.reshape(t, 1).astype(F32)
    inv_r = ROPE_THETA ** (-jnp.arange(0, RET_HEAD_DIM, 2, dtype=F32) / RET_HEAD_DIM)
    inv_m = ROPE_THETA ** (-jnp.arange(0, MLA_QK_ROPE, 2, dtype=F32) / MLA_QK_ROPE)
    z = lambda n: jnp.zeros((n,), F32)
    o = lambda n: jnp.ones((n,), F32)
    inv = jnp.concatenate([inv_r, inv_m, z(LANES - 48)]).reshape(1, LANES)
    sg = jnp.stack([
        jnp.concatenate([-o(64), o(64)]),
        jnp.concatenate([-o(16), z(48), o(16), z(48)]),
    ])
    tm = 512
    tab = jax.ShapeDtypeStruct((t, LANES), F32)
    spec = pl.BlockSpec((tm, LANES), lambda i: (i, 0))
    return pl.pallas_call(
        _rope_tab_kernel,
        grid=(t // tm,),
        in_specs=[pl.BlockSpec((tm, 1), lambda i: (i, 0)), pl.BlockSpec((1, LANES), lambda i: (0, 0)),
                  pl.BlockSpec((2, LANES), lambda i: (0, 0))],
        out_specs=[spec] * 4,
        out_shape=[tab] * 4,
        compiler_params=_cparams(("parallel",)),
        name="rope_tables",
    )(pos, inv, sg)


def _rms(x, g):
    return x * lax.rsqrt(jnp.mean(x * x, axis=-1, keepdims=True) + EPS) * g


def _rot(x, c, s):
    return x * c + pltpu.roll(x, 64, 1) * s


def _proj_kernel(x_ref, sc_ref, sh_ref, g1_ref, wa_ref, wr_ref, gq_ref, wuq_ref, gkv_ref,
                 wuk_ref, wuv_ref, cr_ref, sr_ref, cm_ref, sm_ref,
                 q_ref, k_ref, v_ref, rq_ref, rk_ref, rv_ref, rg_ref):
    x = x_ref[...]
    h = _rms(x, g1_ref[...]) * (1.0 + sc_ref[...]) + sh_ref[...]
    hb = h.astype(BF16)
    cm, sm = cm_ref[...], sm_ref[...]
    cr, sr = cr_ref[...], sr_ref[...]

    pa = jnp.dot(hb, wa_ref[...], preferred_element_type=F32)
    q_lat = pa[:, :MLA_Q_RANK]
    kv_lat = pa[:, MLA_Q_RANK:MLA_Q_RANK + MLA_KV_RANK]
    kr = _rot(pa[:, MLA_Q_RANK + MLA_KV_RANK:], cm, sm)

    qn = _rms(q_lat, gq_ref[...]).astype(BF16)
    q = jnp.dot(qn, wuq_ref[...], preferred_element_type=F32)
    scale = MLA_QK_DIM ** -0.5 * math.log2(math.e)
    for hd in range(MLA_HEADS):
        sl = slice(hd * LANES, (hd + 1) * LANES)
        q_ref[:, sl] = (_rot(q[:, sl], cm, sm) * scale).astype(BF16)

    kvn = _rms(kv_lat, gkv_ref[...]).astype(BF16)
    k = jnp.dot(kvn, wuk_ref[...], preferred_element_type=F32)
    for hd in range(MLA_HEADS):
        sl = slice(hd * LANES, (hd + 1) * LANES)
        k_ref[:, sl] = (k[:, sl] + kr).astype(BF16)
    v = jnp.dot(kvn, wuv_ref[...], preferred_element_type=F32)
    vlane = lax.broadcasted_iota(jnp.int32, v.shape, 1) & (LANES - 1)
    v_ref[...] = jnp.where(vlane == MLA_V_DIM, 1.0, v).astype(BF16)

    pr = jnp.dot(hb, wr_ref[...], preferred_element_type=F32)
    w = RET_WIDTH
    for p in range(RET_HEADS // 2):
        sl = slice(p * LANES, (p + 1) * LANES)
        rq_ref[:, sl] = _rot(pr[:, p * LANES:(p + 1) * LANES], cr, sr).astype(BF16)
        rk_ref[:, sl] = (_rot(pr[:, w + p * LANES:w + (p + 1) * LANES], cr, sr)
                         * (RET_HEAD_DIM ** -0.5)).astype(BF16)
    rv_ref[...] = pr[:, 2 * w:3 * w].astype(BF16)
    rg_ref[...] = pr[:, 3 * w:]


def _proj(x2, sc1, sh1, g1, wa, wr, gq, wuq, gkv, wuk, wuv, tabs, seq):
    t, d = x2.shape
    tm = 512
    tpb = seq // tm
    cr, sr, cm, sm = tabs
    row = lambda n: pl.BlockSpec((tm, n), lambda i: (i, 0))
    full = lambda a: pl.BlockSpec(a.shape, lambda i: (0,) * a.ndim)
    mod = pl.BlockSpec((None, 1, d), lambda i: (i // tpb, 0, 0))
    outs = [(8 * LANES, BF16), (8 * LANES, BF16), (8 * LANES, BF16), (512, BF16), (512, BF16),
            (512, BF16), (512, F32)]
    return pl.pallas_call(
        _proj_kernel,
        grid=(t // tm,),
        in_specs=[row(d), mod, mod, full(g1), full(wa), full(wr), full(gq), full(wuq), full(gkv),
                  full(wuk), full(wuv), row(LANES), row(LANES), row(LANES), row(LANES)],
        out_specs=[row(n) for n, _ in outs],
        out_shape=[jax.ShapeDtypeStruct((t, n), dt) for n, dt in outs],
        compiler_params=_cparams(("parallel",)),
        name="in_proj",
    )(x2, sc1, sh1, g1, wa, wr, gq, wuq, gkv, wuk, wuv, cr, sr, cm, sm)


def _softmax_update(q, kb, vb, m, acc, mask=None):
    s = lax.dot_general(q, kb, _NT, preferred_element_type=F32)
    if mask is not None:
        s = jnp.where(mask, s, -1e30)
    m_new = jnp.maximum(m, jnp.max(s, axis=1, keepdims=True))
    p = jnp.exp2(s - m_new)
    acc = jnp.exp2(m - m_new) * acc + jnp.dot(p.astype(BF16), vb, preferred_element_type=F32)
    return m_new, acc


def _attn_kernel(q_ref, k_ref, v_ref, o_ref, *, tq):
    i = pl.program_id(2)
    half = tq // 2
    heads = [slice(hh * LANES, (hh + 1) * LANES) for hh in range(2)]
    qs = [q_ref[:, hs] for hs in heads]

    def full_block(j, carry):
        off = pl.multiple_of(j * tq, tq)
        return tuple(_softmax_update(q, k_ref[pl.ds(off, tq), hs], v_ref[pl.ds(off, tq), hs], m, acc)
                     for hs, q, (m, acc) in zip(heads, qs, carry))

    carry = ((jnp.full((tq, 1), -1e30, F32), jnp.zeros((tq, LANES), F32)),) * 2
    carry = lax.fori_loop(0, i, full_block, carry)

    off = pl.multiple_of(i * tq, tq)
    tri = (lax.broadcasted_iota(jnp.int32, (half, half), 1) <= lax.broadcasted_iota(jnp.int32, (half, half), 0))
    low = (lax.broadcasted_iota(jnp.int32, (half, tq), 1) <= lax.broadcasted_iota(jnp.int32, (half, tq), 0) + half)
    outs = []
    for hs, q, (m, acc) in zip(heads, qs, carry):
        m0, a0 = _softmax_update(q[:half], k_ref[pl.ds(off, half), hs], v_ref[pl.ds(off, half), hs],
                                 m[:half], acc[:half], tri)
        m1, a1 = _softmax_update(q[half:], k_ref[pl.ds(off, tq), hs], v_ref[pl.ds(off, tq), hs],
                                 m[half:], acc[half:], low)
        acc = jnp.concatenate([a0, a1], axis=0)
        outs.append(acc[:, :MLA_V_DIM] / acc[:, MLA_V_DIM:MLA_V_DIM + 1])
    o_ref[...] = jnp.concatenate(outs, axis=1).astype(o_ref.dtype)


def _attention(q, k, v):
    b, s, _ = q.shape
    tq = 1024
    return pl.pallas_call(
        functools.partial(_attn_kernel, tq=tq),
        grid=(b, MLA_HEADS // 2, s // tq),
        in_specs=[
            pl.BlockSpec((None, tq, 2 * LANES), lambda bi, p, i: (bi, i, p)),
            pl.BlockSpec((None, s, 2 * LANES), lambda bi, p, i: (bi, 0, p)),
            pl.BlockSpec((None, s, 2 * LANES), lambda bi, p, i: (bi, 0, p)),
        ],
        out_specs=pl.BlockSpec((None, tq, LANES), lambda bi, p, i: (bi, i, p)),
        out_shape=jax.ShapeDtypeStruct((b, s, MLA_HEADS * MLA_V_DIM), BF16),
        compiler_params=_cparams(("parallel", "parallel", "arbitrary")),
        name="mla_attention",
    )(q, k, v)


def _ret_kernel(q_ref, k_ref, v_ref, g_ref, dm_ref, z_ref, xi_ref, dec_ref, gn_ref, o_ref, st_ref,
                *, nchunk):
    c = RET_CHUNK

    @pl.when(pl.program_id(1) == 0)
    def _():
        st_ref[...] = jnp.zeros_like(st_ref)

    lane = lax.broadcasted_iota(jnp.int32, (c, LANES), 1)
    sub = lax.broadcasted_iota(jnp.int32, (c, LANES), 0)
    v_first = lane < RET_HEAD_DIM
    k_first = (lane & 32) == 0
    same_head = ((sub & 32) == 0) == v_first

    def head_mean(a):
        tot = jnp.sum(a, axis=1, keepdims=True)
        first = jnp.sum(jnp.where(v_first, a, 0.0), axis=1, keepdims=True)
        return jnp.where(v_first, first, tot - first) * (1.0 / RET_HEAD_DIM)

    for ci in range(nchunk):
        sl = slice(ci * c, (ci + 1) * c)
        for p in range(RET_HEADS // 2):
            ps = slice(p * LANES, (p + 1) * LANES)
            q, k, v = q_ref[sl, ps], k_ref[sl, ps], v_ref[sl, ps]
            zero = jnp.zeros_like(q)
            s_a = lax.dot_general(jnp.where(k_first, q, zero), k, _NT, preferred_element_type=F32) * dm_ref[2 * p]
            s_b = lax.dot_general(jnp.where(k_first, zero, q), k, _NT, preferred_element_type=F32) * dm_ref[2 * p + 1]
            s_ab = jnp.concatenate([s_a, s_b], axis=1).astype(BF16)
            v_bd = jnp.concatenate([jnp.where(v_first, v, zero), jnp.where(v_first, zero, v)], axis=0)
            y = jnp.dot(s_ab, v_bd, preferred_element_type=F32)

            st = st_ref[p]
            st_hi = st.astype(BF16)
            st_lo = (st - st_hi.astype(F32)).astype(BF16)
            y = y + (jnp.dot(q, st_hi, preferred_element_type=F32)
                     + jnp.dot(q, st_lo, preferred_element_type=F32)) * xi_ref[p]

            vz = (v.astype(F32) * z_ref[p]).astype(BF16)
            kt = k.astype(F32).T.astype(BF16)
            kv = jnp.dot(kt, vz, preferred_element_type=F32)
            st_ref[p] = st * dec_ref[p] + jnp.where(same_head, kv, 0.0)

            dlt = y - head_mean(y)
            yn = dlt * lax.rsqrt(head_mean(dlt * dlt) + EPS) * gn_ref[:, ps]
            gate = g_ref[sl, ps]
            o_ref[sl, ps] = (gate * jax.nn.sigmoid(gate) * yn).astype(o_ref.dtype)


def _retention_consts():
    h, c = RET_HEADS, RET_CHUNK
    gamma = 1.0 - 2.0 ** (-5.0 - jnp.arange(h, dtype=F32))
    log_g = jnp.log(gamma)
    idx = jnp.arange(c, dtype=F32)
    diff = idx[:, None] - idx[None, :]
    dmask = jnp.where(diff >= 0, jnp.exp(log_g[:, None, None] * jnp.maximum(diff, 0.0)), 0.0)
    zeta = jnp.exp(log_g[:, None] * (c - 1.0 - idx))
    xi = jnp.exp(log_g[:, None] * (idx + 1.0))
    decay = jnp.exp(log_g * c)
    by_lane = lambda a: jnp.repeat(a.reshape(h // 2, 2, -1), RET_HEAD_DIM, axis=1)
    z = by_lane(zeta).transpose(0, 2, 1)
    x = by_lane(xi).transpose(0, 2, 1)
    dec = by_lane(decay[:, None]).transpose(0, 2, 1)
    return dmask, z, x, dec


def _retention(rq, rk, rv, rg, g_ret):
    b, s, w = rq.shape
    tc = 512
    consts = _retention_consts()
    blk = pl.BlockSpec((None, tc, w), lambda bi, t: (bi, t, 0))
    full = lambda a: pl.BlockSpec(a.shape, lambda bi, t: (0,) * a.ndim)
    gn = g_ret.reshape(1, w)
    return pl.pallas_call(
        functools.partial(_ret_kernel, nchunk=tc // RET_CHUNK),
        grid=(b, s // tc),
        in_specs=[blk, blk, blk, blk] + [full(a) for a in consts] + [full(gn)],
        out_specs=blk,
        out_shape=jax.ShapeDtypeStruct((b, s, w), BF16),
        scratch_shapes=[pltpu.VMEM((RET_HEADS // 2, LANES, LANES), F32)],
        compiler_params=_cparams(("parallel", "arbitrary")),
        name="retention",
    )(rq, rk, rv, rg, *consts, gn)


def _mix_kernel(ym_ref, yr_ref, x_ref, gt_ref, sc_ref, sh_ref, g2_ref, wo_ref, wq_ref, keys_ref,
                x1_ref, h2_ref, st_ref):
    half = ym_ref.shape[1]
    mixed = (jnp.dot(ym_ref[...], wo_ref[:half, :], preferred_element_type=F32)
             + jnp.dot(yr_ref[...], wo_ref[half:, :], preferred_element_type=F32))
    x1 = x_ref[...] + gt_ref[...] * mixed
    x1_ref[...] = x1
    h2 = _rms(x1, g2_ref[...]) * (1.0 + sc_ref[...]) + sh_ref[...]
    h2_ref[...] = h2
    pq = jnp.dot(h2.astype(BF16), wq_ref[...], preferred_element_type=F32)
    for g in range(2 * PEER_HEADS):
        qg = pq[:, g * PEER_HALF:(g + 1) * PEER_HALF].astype(BF16)
        st_ref[g * PEER_N_KEYS:(g + 1) * PEER_N_KEYS, :] = lax.dot_general(
            keys_ref[g % 2], qg, _NT, preferred_element_type=F32)


def _mix(ym, yr, x2, gt1, sc2, sh2, g2, wo, wq, keys, seq):
    t, d = x2.shape
    tm = 512
    tpb = seq // tm
    row = lambda n: pl.BlockSpec((tm, n), lambda i: (i, 0))
    full = lambda a: pl.BlockSpec(a.shape, lambda i: (0,) * a.ndim)
    mod = pl.BlockSpec((None, 1, d), lambda i: (i // tpb, 0, 0))
    ns = 2 * PEER_HEADS * PEER_N_KEYS
    return pl.pallas_call(
        _mix_kernel,
        grid=(t // tm,),
        in_specs=[row(ym.shape[1]), row(yr.shape[1]), row(d), mod, mod, mod, full(g2), full(wo),
                  full(wq), full(keys)],
        out_specs=[row(d), row(d), pl.BlockSpec((ns, tm), lambda i: (0, i))],
        out_shape=[jax.ShapeDtypeStruct((t, d), F32), jax.ShapeDtypeStruct((t, d), F32),
                   jax.ShapeDtypeStruct((ns, t), F32)],
        compiler_params=_cparams(("parallel",)),
        name="out_proj_peer_scores",
    )(ym, yr, x2, gt1, sc2, sh2, g2, wo, wq, keys)


def _top16(s, payload=None):
    rows = lax.broadcasted_iota(jnp.int32, s.shape, 0).astype(F32)
    vals, sel = [], []
    for _ in range(PEER_TOPK):
        m = jnp.max(s, axis=0, keepdims=True)
        at = jnp.min(jnp.where(s == m, rows, float(s.shape[0])), axis=0, keepdims=True)
        hit = rows == at
        vals.append(m)
        sel.append(at if payload is None else jnp.max(jnp.where(hit, payload, -1.0), axis=0, keepdims=True))
        s = jnp.where(hit, -jnp.inf, s)
    return jnp.concatenate(vals, axis=0), jnp.concatenate(sel, axis=0)


def _pair_grid(r0, r1, combine, fill):
    k = PEER_TOPK
    sub = lax.broadcasted_iota(jnp.int32, (8, r0.shape[1]), 0)
    parts = [combine(r0[0:1], r1), combine(r0[1:2], r1[0:8])]
    for a in range(2, 8):
        parts.append(jnp.where(sub < k // (a + 1), combine(r0[a:a + 1], r1[0:8]), fill))
    parts.append(combine(r0[8:16], r1[0:1]))
    return jnp.concatenate(parts, axis=0)


def _topk_kernel(st_ref, e_ref, g_ref, es_ref, gs_ref):
    nk, k = PEER_N_KEYS, PEER_TOPK

    def head(h, _):
        base = pl.multiple_of(h * 2 * nk, 2 * nk)
        v0, i0 = _top16(st_ref[pl.ds(base, nk), :])
        v1, i1 = _top16(st_ref[pl.ds(base + nk, nk), :])
        cand = _pair_grid(v0, v1, lambda x, y: x + y, -jnp.inf)
        cidx = _pair_grid(i0, i1, lambda x, y: x * float(nk) + y, 0.0)
        best, eidx = _top16(cand, cidx)
        ex = jnp.exp(best - jnp.max(best, axis=0, keepdims=True))
        gate = ex / jnp.sum(ex, axis=0, keepdims=True)
        row = pl.multiple_of(h * k, k)
        es_ref[pl.ds(row, k), :] = eidx.astype(jnp.int32)
        gs_ref[pl.ds(row, k), :] = gate
        return 0

    lax.fori_loop(0, PEER_HEADS, head, 0)
    e_ref[...] = es_ref[...].T * ROWS_PER_EXPERT
    g_ref[...] = gs_ref[...].T


def _topk(st):
    ns, t = st.shape
    tt = 512
    out = pl.BlockSpec((tt, PEER_SLOTS), lambda i: (i, 0))
    return pl.pallas_call(
        _topk_kernel,
        grid=(t // tt,),
        in_specs=[pl.BlockSpec((ns, tt), lambda i: (0, i))],
        out_specs=[out, out],
        out_shape=[jax.ShapeDtypeStruct((t, PEER_SLOTS), jnp.int32),
                   jax.ShapeDtypeStruct((t, PEER_SLOTS), F32)],
        scratch_shapes=[pltpu.VMEM((PEER_SLOTS, tt), jnp.int32), pltpu.VMEM((PEER_SLOTS, tt), F32)],
        compiler_params=_cparams(("parallel",)),
        name="peer_topk",
    )(st)


PEER_TB = 128
ROWS_PER_EXPERT = 4
TILE_ROWS = PEER_SLOTS * ROWS_PER_EXPERT


def _pack_kernel(x_ref, o_ref):
    bits = pltpu.bitcast(x_ref[...], jnp.uint32)
    rne = bits + jnp.uint32(0x7FFF) + ((bits >> 16) & jnp.uint32(1))
    half = x_ref.shape[1] // 2
    word = (rne[:, :half] >> 16) | (rne[:, half:] & jnp.uint32(0xFFFF0000))
    o_ref[...] = pltpu.bitcast(word, jnp.int32).reshape(o_ref.shape)


def _pack_table(tab):
    e, d = tab.shape
    te = 256
    return pl.pallas_call(
        _pack_kernel,
        grid=(e // te,),
        in_specs=[pl.BlockSpec((te, d), lambda i: (i, 0))],
        out_specs=pl.BlockSpec((te * ROWS_PER_EXPERT, LANES), lambda i: (i, 0)),
        out_shape=jax.ShapeDtypeStruct((e * ROWS_PER_EXPERT, LANES), jnp.int32),
        compiler_params=_cparams(("parallel",)),
        name="pack_table",
    )(tab)


def _peer_layout():
    j = np.arange(2 * TILE_ROWS)
    chunk = (j % 8) // 2 + 4 * (j % 2)
    mask8 = (chunk[None, :] == np.arange(8)[:, None]).astype(np.float32)
    group = (j[:, None] // 8 == np.arange(PEER_SLOTS)[None, :]).astype(np.float32)
    return jnp.asarray(mask8), jnp.asarray(group, BF16), jnp.asarray(group.T, BF16)


def _load_table(tab_hbm, tab_vmem, sem):
    @pl.when(pl.program_id(0) == 0)
    def _():
        cp = pltpu.make_async_copy(tab_hbm, tab_vmem, sem)
        cp.start()
        cp.wait()


PEER_GROUP = 8
PEER_GROUPS_PER_STEP = 16


def _for_groups(group):
    def step(i, _):
        for u in range(PEER_GROUPS_PER_STEP):
            group(i * PEER_GROUPS_PER_STEP + u)
        return 0

    lax.fori_loop(0, PEER_TB // (PEER_GROUP * PEER_GROUPS_PER_STEP), step, 0)


def _token_tile(idx_ref, tab_ref, token):
    tok_idx = idx_ref.at[pl.ds(token * PEER_SLOTS, PEER_SLOTS)]
    slabs = [tab_ref[pl.ds(pl.multiple_of(tok_idx[s], ROWS_PER_EXPERT), ROWS_PER_EXPERT), :]
             for s in range(PEER_SLOTS)]
    return pltpu.bitcast(jnp.concatenate(slabs, axis=0), BF16)


def _split_bf16(a):
    hi = a.astype(BF16)
    lo = (a - hi.astype(F32)).astype(BF16)
    return jnp.concatenate([hi, lo], axis=0)


def _dot_hilo(a, b01):
    hi = a.astype(BF16)
    lo = (a - hi.astype(F32)).astype(BF16)
    return jnp.dot(hi, b01, preferred_element_type=F32) + jnp.dot(lo, b01, preferred_element_type=F32)


def _peer_u_kernel(idx_ref, x_ref, mask_ref, tab_hbm, o_ref, tab_ref, sem):
    _load_table(tab_hbm, tab_ref, sem)
    mask8 = mask_ref[...]

    def group(t8):
        rows8 = pl.ds(pl.multiple_of(t8 * PEER_GROUP, PEER_GROUP), PEER_GROUP)
        x8 = x_ref[rows8, :].reshape(PEER_GROUP, 8, LANES)
        rows = []
        for j in range(PEER_GROUP):
            tile = _token_tile(idx_ref, tab_ref, t8 * PEER_GROUP + j)
            d = lax.dot_general(_split_bf16(x8[j]), tile, _NT, preferred_element_type=F32)
            rows.append(jnp.sum((d[:8] + d[8:]) * mask8, axis=0, keepdims=True))
        o_ref[rows8, :] = jnp.concatenate(rows, axis=0)

    _for_groups(group)


def _peer_v_kernel(idx_ref, a_ref, g_ref, mask_ref, grp_ref, grpt_ref, x1_ref, gt_ref, gfin_ref, tab_hbm, o_ref,
                   tab_ref, wx_ref, sem, *, final):
    _load_table(tab_hbm, tab_ref, sem)
    mask8 = mask_ref[...]
    act = _dot_hilo(a_ref[...], grp_ref[...])
    gelu = 0.5 * act * (1.0 + lax.erf(act * math.sqrt(0.5)))
    wx_ref[...] = _dot_hilo(g_ref[...] * gelu, grpt_ref[...])

    def group(t8):
        rows8 = pl.ds(pl.multiple_of(t8 * PEER_GROUP, PEER_GROUP), PEER_GROUP)
        w8 = wx_ref[rows8, :]
        outs = []
        for j in range(PEER_GROUP):
            lhs = _split_bf16(w8[j:j + 1, :] * mask8)
            tile = _token_tile(idx_ref, tab_ref, t8 * PEER_GROUP + j)
            out = jnp.dot(lhs, tile, preferred_element_type=F32)
            outs.append(out[:8] + out[8:])
        o_ref[rows8, :] = jnp.stack(outs, axis=0).reshape(PEER_GROUP, 8 * LANES)

    _for_groups(group)
    x = x1_ref[...] + gt_ref[...] * o_ref[...]
    o_ref[...] = _rms(x, gfin_ref[...]) if final else x


def _table_scratch():
    return pltpu.VMEM((PEER_N_KEYS * PEER_N_KEYS * ROWS_PER_EXPERT, LANES), jnp.int32)


def _peer_apply(h2, eidx4, gate, tab_u, tab_v, x1, gt2, g_final, seq, final):
    t, d = h2.shape
    tb = PEER_TB
    mask8, grp, grpt = _peer_layout()
    idx = eidx4.reshape(t * PEER_SLOTS)
    idx_spec = pl.BlockSpec((tb * PEER_SLOTS,), lambda i: (i,), memory_space=pltpu.SMEM)
    full = lambda a: pl.BlockSpec(a.shape, lambda i: (0,) * a.ndim)
    tok_rows = pl.BlockSpec((tb, d), lambda i: (i, 0))
    hbm = pl.BlockSpec(memory_space=pl.ANY)
    act = pl.pallas_call(
        _peer_u_kernel,
        grid=(t // tb,),
        in_specs=[idx_spec, tok_rows, full(mask8), hbm],
        out_specs=pl.BlockSpec((tb, 2 * TILE_ROWS), lambda i: (i, 0)),
        out_shape=jax.ShapeDtypeStruct((t, 2 * TILE_ROWS), F32),
        scratch_shapes=[_table_scratch(), pltpu.SemaphoreType.DMA(())],
        compiler_params=_cparams(("arbitrary",)),
        name="peer_u",
    )(idx, h2, mask8, tab_u)
    tpb = seq // tb
    gfin = g_final.reshape(1, d)
    out = pl.pallas_call(
        functools.partial(_peer_v_kernel, final=final),
        grid=(t // tb,),
        in_specs=[idx_spec, pl.BlockSpec((tb, 2 * TILE_ROWS), lambda i: (i, 0)),
                  pl.BlockSpec((tb, PEER_SLOTS), lambda i: (i, 0)), full(mask8), full(grp), full(grpt),
                  tok_rows, pl.BlockSpec((None, 1, d), lambda i: (i // tpb, 0, 0)), full(gfin), hbm],
        out_specs=tok_rows,
        out_shape=jax.ShapeDtypeStruct((t, d), F32),
        scratch_shapes=[_table_scratch(), pltpu.VMEM((tb, 2 * TILE_ROWS), F32),
                        pltpu.SemaphoreType.DMA(())],
        compiler_params=_cparams(("arbitrary",)),
        name="peer_v",
    )(idx, act, gate, mask8, grp, grpt, x1, gt2, gfin, tab_v)
    return out


def _mla_head_cols(rope_cols, nope_cols):
    pad = lambda n: [-1] * n
    r1 = list(rope_cols[:16]) if rope_cols is not None else pad(16)
    r2 = list(rope_cols[16:]) if rope_cols is not None else pad(16)
    n1 = list(nope_cols[:48]) if nope_cols is not None else pad(48)
    n2 = list(nope_cols[48:]) if nope_cols is not None else pad(16)
    return r1 + n1 + r2 + n2 + pad(32)


def _take_cols(w, cols):
    cols = np.asarray(cols)
    out = jnp.take(w, jnp.asarray(np.maximum(cols, 0)), axis=1)
    return jnp.where(jnp.asarray(cols >= 0)[None, :], out, 0.0)


def _layer_weights(w_in, w_uq, w_ukv):
    qk = MLA_QK_DIM
    uq_cols, uk_cols, uv_cols = [], [], []
    for h in range(MLA_HEADS):
        uq_cols += _mla_head_cols(range(h * qk + MLA_QK_NOPE, (h + 1) * qk), range(h * qk, h * qk + MLA_QK_NOPE))
        kv0 = h * (MLA_QK_NOPE + MLA_V_DIM)
        uk_cols += _mla_head_cols(None, range(kv0, kv0 + MLA_QK_NOPE))
        uv_cols += list(range(kv0 + MLA_QK_NOPE, kv0 + MLA_QK_NOPE + MLA_V_DIM)) + [-1] * (LANES - MLA_V_DIM)
    o_kr = MLA_Q_RANK + MLA_KV_RANK
    o_r = o_kr + MLA_QK_ROPE
    a_cols = list(range(o_kr)) + _mla_head_cols(range(o_kr, o_r), None)

    def pair_cols(base):
        cols = []
        for p in range(RET_HEADS // 2):
            a, b = base + 2 * p * RET_HEAD_DIM, base + (2 * p + 1) * RET_HEAD_DIM
            cols += list(range(a, a + 32)) + list(range(b, b + 32)) + list(range(a + 32, a + 64)) + list(range(b + 32, b + 64))
        return cols

    r_cols = (pair_cols(o_r) + pair_cols(o_r + RET_WIDTH)
              + list(range(o_r + 2 * RET_WIDTH, o_r + 4 * RET_WIDTH)))
    bf = lambda a: a.astype(BF16)
    return (bf(_take_cols(w_in, a_cols)), bf(_take_cols(w_in, r_cols)), bf(_take_cols(w_uq, uq_cols)),
            bf(_take_cols(w_ukv, uk_cols)), bf(_take_cols(w_ukv, uv_cols)))


def kernel(x, c, positions, w_ada, b_ada, g_norm1, w_in, g_q_norm, w_uq, g_kv_norm, w_ukv, g_ret_norm,
           w_out, g_norm2, w_query, sub_keys, expert_u, expert_v, g_final):
    b, s, d = x.shape
    t = b * s
    depth = w_ada.shape[0]
    tabs = _rope_tables(positions)
    x2 = x.reshape(t, d)
    for l in range(depth):
        mod = _adaln(c, w_ada[l], b_ada[l])
        sh1, sc1, gt1, sh2, sc2, gt2 = [m.reshape(b, 1, d) for m in jnp.split(mod, 6, axis=-1)]
        wa, wr, wuq, wuk, wuv = _layer_weights(w_in[l], w_uq[l], w_ukv[l])
        q, k, v, rq, rk, rv, rg = _proj(
            x2, sc1, sh1, g_norm1[l].reshape(1, d), wa, wr, g_q_norm[l].reshape(1, -1), wuq,
            g_kv_norm[l].reshape(1, -1), wuk, wuv, tabs, s)
        r3 = lambda a: a.reshape(b, s, a.shape[-1])
        y_mla = _attention(r3(q), r3(k), r3(v)).reshape(t, -1)
        y_ret = _retention(r3(rq), r3(rk), r3(rv), r3(rg), g_ret_norm[l]).reshape(t, -1)
        x1, h2, st = _mix(y_mla, y_ret, x2, gt1, sc2, sh2, g_norm2[l].reshape(1, d),
                          w_out[l].astype(BF16), w_query[l].astype(BF16), sub_keys[l].astype(BF16), s)
        eidx4, gate = _topk(st)
        x2 = _peer_apply(h2, eidx4, gate, _pack_table(expert_u[l]), _pack_table(expert_v[l]),
                         x1, gt2, g_final, s, final=(l == depth - 1))
    return x2.reshape(b, s, d)
```

```python
import functools
import math

import jax
import jax.numpy as jnp
import numpy as np
from jax import lax
from jax.experimental import pallas as pl
from jax.experimental.pallas import tpu as pltpu

F32 = jnp.float32
BF16 = jnp.bfloat16
HIGHEST = lax.Precision.HIGHEST

EPS = 1e-6
ROPE_THETA = 10000.0

MLA_HEADS = 8
MLA_QK_NOPE = 64
MLA_QK_ROPE = 32
MLA_QK_DIM = MLA_QK_NOPE + MLA_QK_ROPE
MLA_V_DIM = 64
MLA_Q_RANK = 256
MLA_KV_RANK = 128

RET_HEADS = 8
RET_HEAD_DIM = 64
RET_WIDTH = RET_HEADS * RET_HEAD_DIM
RET_CHUNK = 128

PEER_HEADS = 8
PEER_N_KEYS = 128
PEER_HALF = 128
PEER_TOPK = 16
PEER_SLOTS = PEER_HEADS * PEER_TOPK

LANES = 128
VMEM_LIMIT = 56 * 1024 * 1024

_NT = (((1,), (1,)), ((), ()))


def _cparams(sem):
    return pltpu.CompilerParams(dimension_semantics=sem, vmem_limit_bytes=VMEM_LIMIT)


def _adaln_kernel(c_ref, w_ref, b_ref, o_ref):
    c = c_ref[...]
    s = c * jax.nn.sigmoid(c)
    o_ref[...] = jnp.dot(s, w_ref[...], precision=HIGHEST, preferred_element_type=F32) + b_ref[...]


def _adaln(c, w, b):
    bsz, d = c.shape
    n = w.shape[1]
    return pl.pallas_call(
        _adaln_kernel,
        grid=(n // d,),
        in_specs=[
            pl.BlockSpec((bsz, d), lambda j: (0, 0)),
            pl.BlockSpec((d, d), lambda j: (0, j)),
            pl.BlockSpec((1, d), lambda j: (0, j)),
        ],
        out_specs=pl.BlockSpec((bsz, d), lambda j: (0, j)),
        out_shape=jax.ShapeDtypeStruct((bsz, n), F32),
        compiler_params=_cparams(("arbitrary",)),
        name="adaln",
    )(c, w, b.reshape(1, n))


def _rope_tab_kernel(pos_ref, inv_ref, sg_ref, cr_ref, sr_ref, cm_ref, sm_ref):
    ang = pos_ref[...] * inv_ref[...]
    cos, sin = jnp.cos(ang), jnp.sin(ang)
    nr, nm = RET_HEAD_DIM // 2, MLA_QK_ROPE // 2
    ret = lambda a: jnp.concatenate([a[:, :nr]] * (LANES // nr), axis=1)
    mla = lambda a, fill: jnp.concatenate(
        [a[:, nr:nr + nm], jnp.full((a.shape[0], 64 - nm), fill, F32)] * 2, axis=1)
    cr_ref[...] = ret(cos)
    sr_ref[...] = ret(sin) * sg_ref[0:1, :]
    cm_ref[...] = mla(cos, 1.0)
    sm_ref[...] = mla(sin, 0.0) * sg_ref[1:2, :]


def _rope_tables(positions):
    t = positions.size
    pos = positions.reshape(t, 1).astype(F32)
    inv_r = ROPE_THETA ** (-jnp.arange(0, RET_HEAD_DIM, 2, dtype=F32) / RET_HEAD_DIM)
    inv_m = ROPE_THETA ** (-jnp.arange(0, MLA_QK_ROPE, 2, dtype=F32) / MLA_QK_ROPE)
    z = lambda n: jnp.zeros((n,), F32)
    o = lambda n: jnp.ones((n,), F32)
    inv = jnp.concatenate([inv_r, inv_m, z(LANES - 48)]).reshape(1, LANES)
    sg = jnp.stack([
        jnp.concatenate([-o(64), o(64)]),
        jnp.concatenate([-o(16), z(48), o(16), z(48)]),
    ])
    tm = 512
    tab = jax.ShapeDtypeStruct((t, LANES), F32)
    spec = pl.BlockSpec((tm, LANES), lambda i: (i, 0))
    return pl.pallas_call(
        _rope_tab_kernel,
        grid=(t // tm,),
        in_specs=[pl.BlockSpec((tm, 1), lambda i: (i, 0)), pl.BlockSpec((1, LANES), lambda i: (0, 0)),
                  pl.BlockSpec((2, LANES), lambda i: (0, 0))],
        out_specs=[spec] * 4,
        out_shape=[tab] * 4,
        compiler_params=_cparams(("parallel",)),
        name="rope_tables",
    )(pos, inv, sg)


def _rms(x, g):
    return x * lax.rsqrt(jnp.mean(x * x, axis=-1, keepdims=True) + EPS) * g


def _rot(x, c, s):
    return x * c + pltpu.roll(x, 64, 1) * s


def _proj_kernel(x_ref, sc_ref, sh_ref, g1_ref, wa_ref, wr_ref, gq_ref, wuq_ref, gkv_ref,
                 wuk_ref, wuv_ref, cr_ref, sr_ref, cm_ref, sm_ref,
                 q_ref, k_ref, v_ref, rq_ref, rk_ref, rv_ref, rg_ref):
    x = x_ref[...]
    h = _rms(x, g1_ref[...]) * (1.0 + sc_ref[...]) + sh_ref[...]
    hb = h.astype(BF16)
    cm, sm = cm_ref[...], sm_ref[...]
    cr, sr = cr_ref[...], sr_ref[...]

    pa = jnp.dot(hb, wa_ref[...], preferred_element_type=F32)
    q_lat = pa[:, :MLA_Q_RANK]
    kv_lat = pa[:, MLA_Q_RANK:MLA_Q_RANK + MLA_KV_RANK]
    kr = _rot(pa[:, MLA_Q_RANK + MLA_KV_RANK:], cm, sm)

    qn = _rms(q_lat, gq_ref[...]).astype(BF16)
    q = jnp.dot(qn, wuq_ref[...], preferred_element_type=F32)
    scale = MLA_QK_DIM ** -0.5 * math.log2(math.e)
    for hd in range(MLA_HEADS):
        sl = slice(hd * LANES, (hd + 1) * LANES)
        q_ref[:, sl] = (_rot(q[:, sl], cm, sm) * scale).astype(BF16)

    kvn = _rms(kv_lat, gkv_ref[...]).astype(BF16)
    k = jnp.dot(kvn, wuk_ref[...], preferred_element_type=F32)
    for hd in range(MLA_HEADS):
        sl = slice(hd * LANES, (hd + 1) * LANES)
        k_ref[:, sl] = (k[:, sl] + kr).astype(BF16)
    v = jnp.dot(kvn, wuv_ref[...], preferred_element_type=F32)
    vlane = lax.broadcasted_iota(jnp.int32, v.shape, 1) & (LANES - 1)
    v_ref[...] = jnp.where(vlane == MLA_V_DIM, 1.0, v).astype(BF16)

    pr = jnp.dot(hb, wr_ref[...], preferred_element_type=F32)
    w = RET_WIDTH
    for p in range(RET_HEADS // 2):
        sl = slice(p * LANES, (p + 1) * LANES)
        rq_ref[:, sl] = _rot(pr[:, p * LANES:(p + 1) * LANES], cr, sr).astype(BF16)
        rk_ref[:, sl] = (_rot(pr[:, w + p * LANES:w + (p + 1) * LANES], cr, sr)
                         * (RET_HEAD_DIM ** -0.5)).astype(BF16)
    rv_ref[...] = pr[:, 2 * w:3 * w].astype(BF16)
    rg_ref[...] = pr[:, 3 * w:]


def _proj(x2, sc1, sh1, g1, wa, wr, gq, wuq, gkv, wuk, wuv, tabs, seq):
    t, d = x2.shape
    tm = 512
    tpb = seq // tm
    cr, sr, cm, sm = tabs
    row = lambda n: pl.BlockSpec((tm, n), lambda i: (i, 0))
    full = lambda a: pl.BlockSpec(a.shape, lambda i: (0,) * a.ndim)
    mod = pl.BlockSpec((None, 1, d), lambda i: (i // tpb, 0, 0))
    outs = [(8 * LANES, BF16), (8 * LANES, BF16), (8 * LANES, BF16), (512, BF16), (512, BF16),
            (512, BF16), (512, F32)]
    return pl.pallas_call(
        _proj_kernel,
        grid=(t // tm,),
        in_specs=[row(d), mod, mod, full(g1), full(wa), full(wr), full(gq), full(wuq), full(gkv),
                  full(wuk), full(wuv), row(LANES), row(LANES), row(LANES), row(LANES)],
        out_specs=[row(n) for n, _ in outs],
        out_shape=[jax.ShapeDtypeStruct((t, n), dt) for n, dt in outs],
        compiler_params=_cparams(("parallel",)),
        name="in_proj",
    )(x2, sc1, sh1, g1, wa, wr, gq, wuq, gkv, wuk, wuv, cr, sr, cm, sm)


def _softmax_update(q, kb, vb, m, acc, mask=None):
    s = lax.dot_general(q, kb, _NT, preferred_element_type=F32)
    if mask is not None:
        s = jnp.where(mask, s, -1e30)
    m_new = jnp.maximum(m, jnp.max(s, axis=1, keepdims=True))
    p = jnp.exp2(s - m_new)
    acc = jnp.exp2(m - m_new) * acc + jnp.dot(p.astype(BF16), vb, preferred_element_type=F32)
    return m_new, acc


def _attn_kernel(q_ref, k_ref, v_ref, o_ref, *, tq):
    i = pl.program_id(2)
    half = tq // 2
    heads = [slice(hh * LANES, (hh + 1) * LANES) for hh in range(2)]
    qs = [q_ref[:, hs] for hs in heads]

    def full_block(j, carry):
        off = pl.multiple_of(j * tq, tq)
        return tuple(_softmax_update(q, k_ref[pl.ds(off, tq), hs], v_ref[pl.ds(off, tq), hs], m, acc)
                     for hs, q, (m, acc) in zip(heads, qs, carry))

    carry = ((jnp.full((tq, 1), -1e30, F32), jnp.zeros((tq, LANES), F32)),) * 2
    carry = lax.fori_loop(0, i, full_block, carry)

    off = pl.multiple_of(i * tq, tq)
    tri = (lax.broadcasted_iota(jnp.int32, (half, half), 1) <= lax.broadcasted_iota(jnp.int32, (half, half), 0))
    low = (lax.broadcasted_iota(jnp.int32, (half, tq), 1) <= lax.broadcasted_iota(jnp.int32, (half, tq), 0) + half)
    outs = []
    for hs, q, (m, acc) in zip(heads, qs, carry):
        m0, a0 = _softmax_update(q[:half], k_ref[pl.ds(off, half), hs], v_ref[pl.ds(off, half), hs],
                                 m[:half], acc[:half], tri)
        m1, a1 = _softmax_update(q[half:], k_ref[pl.ds(off, tq), hs], v_ref[pl.ds(off, tq), hs],
                                 m[half:], acc[half:], low)
        acc = jnp.concatenate([a0, a1], axis=0)
        outs.append(acc[:, :MLA_V_DIM] / acc[:, MLA_V_DIM:MLA_V_DIM + 1])
    o_ref[...] = jnp.concatenate(outs, axis=1).astype(o_ref.dtype)


def _attention(q, k, v):
    b, s, _ = q.shape
    tq = 1024
    return pl.pallas_call(
        functools.partial(_attn_kernel, tq=tq),
        grid=(b, MLA_HEADS // 2, s // tq),
        in_specs=[
            pl.BlockSpec((None, tq, 2 * LANES), lambda bi, p, i: (bi, i, p)),
            pl.BlockSpec((None, s, 2 * LANES), lambda bi, p, i: (bi, 0, p)),
            pl.BlockSpec((None, s, 2 * LANES), lambda bi, p, i: (bi, 0, p)),
        ],
        out_specs=pl.BlockSpec((None, tq, LANES), lambda bi, p, i: (bi, i, p)),
        out_shape=jax.ShapeDtypeStruct((b, s, MLA_HEADS * MLA_V_DIM), BF16),
        compiler_params=_cparams(("parallel", "parallel", "arbitrary")),
        name="mla_attention",
    )(q, k, v)


def _ret_kernel(q_ref, k_ref, v_ref, g_ref, dm_ref, z_ref, xi_ref, dec_ref, gn_ref, o_ref, st_ref,
                *, nchunk):
    c = RET_CHUNK

    @pl.when(pl.program_id(1) == 0)
    def _():
        st_ref[...] = jnp.zeros_like(st_ref)

    lane = lax.broadcasted_iota(jnp.int32, (c, LANES), 1)
    sub = lax.broadcasted_iota(jnp.int32, (c, LANES), 0)
    v_first = lane < RET_HEAD_DIM
    k_first = (lane & 32) == 0
    same_head = ((sub & 32) == 0) == v_first

    def head_mean(a):
        tot = jnp.sum(a, axis=1, keepdims=True)
        first = jnp.sum(jnp.where(v_first, a, 0.0), axis=1, keepdims=True)
        return jnp.where(v_first, first, tot - first) * (1.0 / RET_HEAD_DIM)

    for ci in range(nchunk):
        sl = slice(ci * c, (ci + 1) * c)
        for p in range(RET_HEADS // 2):
            ps = slice(p * LANES, (p + 1) * LANES)
            q, k, v = q_ref[sl, ps], k_ref[sl, ps], v_ref[sl, ps]
            zero = jnp.zeros_like(q)
            s_a = lax.dot_general(jnp.where(k_first, q, zero), k, _NT, preferred_element_type=F32) * dm_ref[2 * p]
            s_b = lax.dot_general(jnp.where(k_first, zero, q), k, _NT, preferred_element_type=F32) * dm_ref[2 * p + 1]
            s_ab = jnp.concatenate([s_a, s_b], axis=1).astype(BF16)
            v_bd = jnp.concatenate([jnp.where(v_first, v, zero), jnp.where(v_first, zero, v)], axis=0)
            y = jnp.dot(s_ab, v_bd, preferred_element_type=F32)

            st = st_ref[p]
            st_hi = st.astype(BF16)
            st_lo = (st - st_hi.astype(F32)).astype(BF16)
            y = y + (jnp.dot(q, st_hi, preferred_element_type=F32)
                     + jnp.dot(q, st_lo, preferred_element_type=F32)) * xi_ref[p]

            vz = (v.astype(F32) * z_ref[p]).astype(BF16)
            kt = k.astype(F32).T.astype(BF16)
            kv = jnp.dot(kt, vz, preferred_element_type=F32)
            st_ref[p] = st * dec_ref[p] + jnp.where(same_head, kv, 0.0)

            dlt = y - head_mean(y)
            yn = dlt * lax.rsqrt(head_mean(dlt * dlt) + EPS) * gn_ref[:, ps]
            gate = g_ref[sl, ps]
            o_ref[sl, ps] = (gate * jax.nn.sigmoid(gate) * yn).astype(o_ref.dtype)


def _retention_consts():
    h, c = RET_HEADS, RET_CHUNK
    gamma = 1.0 - 2.0 ** (-5.0 - jnp.arange(h, dtype=F32))
    log_g = jnp.log(gamma)
    idx = jnp.arange(c, dtype=F32)
    diff = idx[:, None] - idx[None, :]
    dmask = jnp.where(diff >= 0, jnp.exp(log_g[:, None, None] * jnp.maximum(diff, 0.0)), 0.0)
    zeta = jnp.exp(log_g[:, None] * (c - 1.0 - idx))
    xi = jnp.exp(log_g[:, None] * (idx + 1.0))
    decay = jnp.exp(log_g * c)
    by_lane = lambda a: jnp.repeat(a.reshape(h // 2, 2, -1), RET_HEAD_DIM, axis=1)
    z = by_lane(zeta).transpose(0, 2, 1)
    x = by_lane(xi).transpose(0, 2, 1)
    dec = by_lane(decay[:, None]).transpose(0, 2, 1)
    return dmask, z, x, dec


def _retention(rq, rk, rv, rg, g_ret):
    b, s, w = rq.shape
    tc = 512
    consts = _retention_consts()
    blk = pl.BlockSpec((None, tc, w), lambda bi, t: (bi, t, 0))
    full = lambda a: pl.BlockSpec(a.shape, lambda bi, t: (0,) * a.ndim)
    gn = g_ret.reshape(1, w)
    return pl.pallas_call(
        functools.partial(_ret_kernel, nchunk=tc // RET_CHUNK),
        grid=(b, s // tc),
        in_specs=[blk, blk, blk, blk] + [full(a) for a in consts] + [full(gn)],
        out_specs=blk,
        out_shape=jax.ShapeDtypeStruct((b, s, w), BF16),
        scratch_shapes=[pltpu.VMEM((RET_HEADS // 2, LANES, LANES), F32)],
        compiler_params=_cparams(("parallel", "arbitrary")),
        name="retention",
    )(rq, rk, rv, rg, *consts, gn)


def _mix_kernel(ym_ref, yr_ref, x_ref, gt_ref, sc_ref, sh_ref, g2_ref, wo_ref, wq_ref, keys_ref,
                x1_ref, h2_ref, st_ref):
    half = ym_ref.shape[1]
    mixed = (jnp.dot(ym_ref[...], wo_ref[:half, :], preferred_element_type=F32)
             + jnp.dot(yr_ref[...], wo_ref[half:, :], preferred_element_type=F32))
    x1 = x_ref[...] + gt_ref[...] * mixed
    x1_ref[...] = x1
    h2 = _rms(x1, g2_ref[...]) * (1.0 + sc_ref[...]) + sh_ref[...]
    h2_ref[...] = h2
    pq = jnp.dot(h2.astype(BF16), wq_ref[...], preferred_element_type=F32)
    for g in range(2 * PEER_HEADS):
        qg = pq[:, g * PEER_HALF:(g + 1) * PEER_HALF].astype(BF16)
        st_ref[g * PEER_N_KEYS:(g + 1) * PEER_N_KEYS, :] = lax.dot_general(
            keys_ref[g % 2], qg, _NT, preferred_element_type=F32)


def _mix(ym, yr, x2, gt1, sc2, sh2, g2, wo, wq, keys, seq):
    t, d = x2.shape
    tm = 512
    tpb = seq // tm
    row = lambda n: pl.BlockSpec((tm, n), lambda i: (i, 0))
    full = lambda a: pl.BlockSpec(a.shape, lambda i: (0,) * a.ndim)
    mod = pl.BlockSpec((None, 1, d), lambda i: (i // tpb, 0, 0))
    ns = 2 * PEER_HEADS * PEER_N_KEYS
    return pl.pallas_call(
        _mix_kernel,
        grid=(t // tm,),
        in_specs=[row(ym.shape[1]), row(yr.shape[1]), row(d), mod, mod, mod, full(g2), full(wo),
                  full(wq), full(keys)],
        out_specs=[row(d), row(d), pl.BlockSpec((ns, tm), lambda i: (0, i))],
        out_shape=[jax.ShapeDtypeStruct((t, d), F32), jax.ShapeDtypeStruct((t, d), F32),
                   jax.ShapeDtypeStruct((ns, t), F32)],
        compiler_params=_cparams(("parallel",)),
        name="out_proj_peer_scores",
    )(ym, yr, x2, gt1, sc2, sh2, g2, wo, wq, keys)


def _top16(s, payload=None, order=None):
    rows = lax.broadcasted_iota(jnp.int32, s.shape, 0).astype(F32) if order is None else order
    vals, sel = [], []
    for _ in range(PEER_TOPK):
        m = jnp.max(s, axis=0, keepdims=True)
        at = jnp.min(jnp.where(s == m, rows, 1e9), axis=0, keepdims=True)
        hit = rows == at
        vals.append(m)
        sel.append(at if payload is None else jnp.max(jnp.where(hit, payload, -1.0), axis=0, keepdims=True))
        s = jnp.where(hit, -jnp.inf, s)
    return jnp.concatenate(vals, axis=0), jnp.concatenate(sel, axis=0)


_SORT16 = ((0, 1), (2, 3), (4, 5), (6, 7), (8, 9), (10, 11), (12, 13), (14, 15), (0, 2), (1, 3), (4, 6), (5, 7),
           (8, 10), (9, 11), (12, 14), (13, 15), (1, 2), (5, 6), (9, 10), (13, 14), (0, 4), (1, 5), (2, 6), (3, 7),
           (8, 12), (9, 13), (10, 14), (11, 15), (2, 4), (3, 5), (10, 12), (11, 13), (1, 2), (3, 4), (5, 6), (9, 10),
           (11, 12), (13, 14), (0, 8), (1, 9), (2, 10), (3, 11), (4, 12), (5, 13), (6, 14), (7, 15), (4, 8), (5, 9),
           (6, 10), (7, 11), (2, 4), (3, 5), (6, 8), (7, 9), (10, 12), (11, 13), (1, 2), (3, 4), (5, 6), (7, 8),
           (9, 10), (11, 12), (13, 14))


def _sorted_top16(s):
    sub = lax.broadcasted_iota(jnp.int32, (8, LANES), 0).astype(F32)
    v = [s[8 * j:8 * j + 8, :] for j in range(16)]
    ix = [sub + float(8 * j) for j in range(16)]

    def exchange(i, j):
        c = v[j] > v[i]
        v[i], v[j] = jnp.maximum(v[i], v[j]), jnp.minimum(v[i], v[j])
        ix[i], ix[j] = jnp.where(c, ix[j], ix[i]), jnp.where(c, ix[i], ix[j])

    for i, j in _SORT16:
        exchange(i, j)
    for d in (4, 2, 1):
        bv = [pltpu.roll(v[15 - i], d, 0) for i in range(16)]
        bi = [pltpu.roll(ix[15 - i], d, 0) for i in range(16)]
        for i in range(16):
            c = bv[i] > v[i]
            v[i] = jnp.maximum(v[i], bv[i])
            ix[i] = jnp.where(c, bi[i], ix[i])
        for stride in (8, 4, 2, 1):
            for i in range(16):
                if i & stride == 0:
                    exchange(i, i + stride)
    vals = jnp.concatenate([t[0:1, :] for t in v], axis=0)
    ids = jnp.concatenate([t[0:1, :] for t in ix], axis=0)
    distinct = jnp.min(jnp.where(vals[:-1] > vals[1:], 1.0, 0.0), axis=0, keepdims=True)
    at_least = jnp.sum(jnp.where(s >= vals[15:16], 1.0, 0.0), axis=0, keepdims=True)
    return vals, ids, (distinct > 0.0) & (at_least == float(PEER_TOPK))


def _pair_grid(r0, r1, combine, fill, roll=pltpu.roll):
    sub = lax.broadcasted_iota(jnp.int32, (8, r0.shape[1]), 0)
    run = lambda a: combine(r0[a:a + 1], r1[0:8])
    shift = lambda x, n: roll(x, n, 0)
    tail = shift(combine(r0[8:16], r1[0:1]), 2)
    return jnp.concatenate([
        combine(r0[0:1], r1), run(1),
        jnp.where(sub < 5, run(2), shift(run(4), 5)),
        jnp.where(sub < 4, run(3), jnp.where(sub < 6, shift(run(5), 4), shift(run(6), 6))),
        jnp.where(sub < 2, run(7), tail),
        jnp.where(sub < 2, tail, fill),
    ], axis=0)


def _topk_kernel(st_ref, flat_ref, e_ref, g_ref, es_ref, gs_ref):
    nk, k = PEER_N_KEYS, PEER_TOPK
    flat = flat_ref[...]

    def stage1(half_ref_rows, fast):
        if not fast:
            return _top16(half_ref_rows) + (None,)
        cols = [_sorted_top16(half_ref_rows[:, c * LANES:(c + 1) * LANES]) for c in range(half_ref_rows.shape[1] // LANES)]
        return tuple(jnp.concatenate([col[n] for col in cols], axis=1) for n in range(3))

    def select(s0, s1, fast):
        v0, i0, ok0 = stage1(s0, fast)
        v1, i1, ok1 = stage1(s1, fast)
        cand = _pair_grid(v0, v1, lambda x, y: x + y, -jnp.inf)
        cidx = _pair_grid(i0, i1, lambda x, y: x * float(nk) + y, 0.0)
        best, eidx = _top16(cand, cidx, order=flat)
        return best, eidx, (jnp.all(ok0 & ok1) if fast else None)

    def head(h, _):
        base = pl.multiple_of(h * 2 * nk, 2 * nk)
        s0, s1 = st_ref[pl.ds(base, nk), :], st_ref[pl.ds(base + nk, nk), :]
        best, eidx, tie_free = select(s0, s1, True)
        best, eidx = lax.cond(tie_free, lambda: (best, eidx), lambda: select(s0, s1, False)[:2])
        ex = jnp.exp(best - jnp.max(best, axis=0, keepdims=True))
        gate = ex / jnp.sum(ex, axis=0, keepdims=True)
        row = pl.multiple_of(h * k, k)
        es_ref[pl.ds(row, k), :] = eidx.astype(jnp.int32)
        gs_ref[pl.ds(row, k), :] = gate
        return 0

    lax.fori_loop(0, PEER_HEADS, head, 0)
    e_ref[...] = es_ref[...].T * ROWS_PER_EXPERT
    g_ref[...] = gs_ref[...].T


def _topk(st):
    ns, t = st.shape
    tt = 512
    out = pl.BlockSpec((tt, PEER_SLOTS), lambda i: (i, 0))
    pos = jnp.broadcast_to(jnp.arange(PEER_TOPK, dtype=F32)[:, None], (PEER_TOPK, tt))
    flat = _pair_grid(pos, pos, lambda x, y: x * float(PEER_TOPK) + y, 1e8, roll=jnp.roll)
    return pl.pallas_call(
        _topk_kernel,
        grid=(t // tt,),
        in_specs=[pl.BlockSpec((ns, tt), lambda i: (0, i)), pl.BlockSpec(flat.shape, lambda i: (0, 0))],
        out_specs=[out, out],
        out_shape=[jax.ShapeDtypeStruct((t, PEER_SLOTS), jnp.int32),
                   jax.ShapeDtypeStruct((t, PEER_SLOTS), F32)],
        scratch_shapes=[pltpu.VMEM((PEER_SLOTS, tt), jnp.int32), pltpu.VMEM((PEER_SLOTS, tt), F32)],
        compiler_params=_cparams(("parallel",)),
        name="peer_topk",
    )(st, flat)


PEER_TB = 128
ROWS_PER_EXPERT = 4
TILE_ROWS = PEER_SLOTS * ROWS_PER_EXPERT


def _pack_kernel(x_ref, o_ref):
    bits = pltpu.bitcast(x_ref[...], jnp.uint32)
    rne = bits + jnp.uint32(0x7FFF) + ((bits >> 16) & jnp.uint32(1))
    half = x_ref.shape[1] // 2
    word = (rne[:, :half] >> 16) | (rne[:, half:] & jnp.uint32(0xFFFF0000))
    o_ref[...] = pltpu.bitcast(word, jnp.int32).reshape(o_ref.shape)


def _pack_table(tab):
    e, d = tab.shape
    te = 256
    return pl.pallas_call(
        _pack_kernel,
        grid=(e // te,),
        in_specs=[pl.BlockSpec((te, d), lambda i: (i, 0))],
        out_specs=pl.BlockSpec((te * ROWS_PER_EXPERT, LANES), lambda i: (i, 0)),
        out_shape=jax.ShapeDtypeStruct((e * ROWS_PER_EXPERT, LANES), jnp.int32),
        compiler_params=_cparams(("parallel",)),
        name="pack_table",
    )(tab)


def _peer_layout():
    j = np.arange(2 * TILE_ROWS)
    chunk = (j % 8) // 2 + 4 * (j % 2)
    mask8 = (chunk[None, :] == np.arange(8)[:, None]).astype(np.float32)
    group = (j[:, None] // 8 == np.arange(PEER_SLOTS)[None, :]).astype(np.float32)
    return jnp.asarray(mask8), jnp.asarray(group, BF16), jnp.asarray(group.T, BF16)


def _load_table(tab_hbm, tab_vmem, sem):
    @pl.when(pl.program_id(0) == 0)
    def _():
        cp = pltpu.make_async_copy(tab_hbm, tab_vmem, sem)
        cp.start()
        cp.wait()


PEER_GROUP = 8
PEER_GROUPS_PER_STEP = 16


def _for_groups(group):
    def step(i, _):
        for u in range(PEER_GROUPS_PER_STEP):
            group(i * PEER_GROUPS_PER_STEP + u)
        return 0

    lax.fori_loop(0, PEER_TB // (PEER_GROUP * PEER_GROUPS_PER_STEP), step, 0)


def _token_tile(idx_ref, tab_ref, token):
    tok_idx = idx_ref.at[pl.ds(token * PEER_SLOTS, PEER_SLOTS)]
    slabs = [tab_ref[pl.ds(pl.multiple_of(tok_idx[s], ROWS_PER_EXPERT), ROWS_PER_EXPERT), :]
             for s in range(PEER_SLOTS)]
    return pltpu.bitcast(jnp.concatenate(slabs, axis=0), BF16)


def _split_bf16(a):
    hi = a.astype(BF16)
    lo = (a - hi.astype(F32)).astype(BF16)
    return jnp.concatenate([hi, lo], axis=0)


def _dot_hilo(a, b01):
    hi = a.astype(BF16)
    lo = (a - hi.astype(F32)).astype(BF16)
    return jnp.dot(hi, b01, preferred_element_type=F32) + jnp.dot(lo, b01, preferred_element_type=F32)


def _peer_u_kernel(idx_ref, x_ref, mask_ref, tab_hbm, o_ref, tab_ref, sem):
    _load_table(tab_hbm, tab_ref, sem)
    mask8 = mask_ref[...]

    def group(t8):
        rows8 = pl.ds(pl.multiple_of(t8 * PEER_GROUP, PEER_GROUP), PEER_GROUP)
        x8 = x_ref[rows8, :].reshape(PEER_GROUP, 8, LANES)
        rows = []
        for j in range(PEER_GROUP):
            tile = _token_tile(idx_ref, tab_ref, t8 * PEER_GROUP + j)
            d = lax.dot_general(_split_bf16(x8[j]), tile, _NT, preferred_element_type=F32)
            rows.append(jnp.sum((d[:8] + d[8:]) * mask8, axis=0, keepdims=True))
        o_ref[rows8, :] = jnp.concatenate(rows, axis=0)

    _for_groups(group)


def _peer_v_kernel(idx_ref, a_ref, g_ref, mask_ref, grp_ref, grpt_ref, x1_ref, gt_ref, gfin_ref, tab_hbm, o_ref,
                   tab_ref, wx_ref, sem, *, final):
    _load_table(tab_hbm, tab_ref, sem)
    mask8 = mask_ref[...]
    act = _dot_hilo(a_ref[...], grp_ref[...])
    gelu = 0.5 * act * (1.0 + lax.erf(act * math.sqrt(0.5)))
    wx_ref[...] = _dot_hilo(g_ref[...] * gelu, grpt_ref[...])

    def group(t8):
        rows8 = pl.ds(pl.multiple_of(t8 * PEER_GROUP, PEER_GROUP), PEER_GROUP)
        w8 = wx_ref[rows8, :]
        outs = []
        for j in range(PEER_GROUP):
            lhs = _split_bf16(w8[j:j + 1, :] * mask8)
            tile = _token_tile(idx_ref, tab_ref, t8 * PEER_GROUP + j)
            out = jnp.dot(lhs, tile, preferred_element_type=F32)
            outs.append(out[:8] + out[8:])
        o_ref[rows8, :] = jnp.stack(outs, axis=0).reshape(PEER_GROUP, 8 * LANES)

    _for_groups(group)
    x = x1_ref[...] + gt_ref[...] * o_ref[...]
    o_ref[...] = _rms(x, gfin_ref[...]) if final else x


def _table_scratch():
    return pltpu.VMEM((PEER_N_KEYS * PEER_N_KEYS * ROWS_PER_EXPERT, LANES), jnp.int32)


def _peer_apply(h2, eidx4, gate, tab_u, tab_v, x1, gt2, g_final, seq, final):
    t, d = h2.shape
    tb = PEER_TB
    mask8, grp, grpt = _peer_layout()
    idx = eidx4.reshape(t * PEER_SLOTS)
    idx_spec = pl.BlockSpec((tb * PEER_SLOTS,), lambda i: (i,), memory_space=pltpu.SMEM)
    full = lambda a: pl.BlockSpec(a.shape, lambda i: (0,) * a.ndim)
    tok_rows = pl.BlockSpec((tb, d), lambda i: (i, 0))
    hbm = pl.BlockSpec(memory_space=pl.ANY)
    act = pl.pallas_call(
        _peer_u_kernel,
        grid=(t // tb,),
        in_specs=[idx_spec, tok_rows, full(mask8), hbm],
        out_specs=pl.BlockSpec((tb, 2 * TILE_ROWS), lambda i: (i, 0)),
        out_shape=jax.ShapeDtypeStruct((t, 2 * TILE_ROWS), F32),
        scratch_shapes=[_table_scratch(), pltpu.SemaphoreType.DMA(())],
        compiler_params=_cparams(("arbitrary",)),
        name="peer_u",
    )(idx, h2, mask8, tab_u)
    tpb = seq // tb
    gfin = g_final.reshape(1, d)
    out = pl.pallas_call(
        functools.partial(_peer_v_kernel, final=final),
        grid=(t // tb,),
        in_specs=[idx_spec, pl.BlockSpec((tb, 2 * TILE_ROWS), lambda i: (i, 0)),
                  pl.BlockSpec((tb, PEER_SLOTS), lambda i: (i, 0)), full(mask8), full(grp), full(grpt),
                  tok_rows, pl.BlockSpec((None, 1, d), lambda i: (i // tpb, 0, 0)), full(gfin), hbm],
        out_specs=tok_rows,
        out_shape=jax.ShapeDtypeStruct((t, d), F32),
        scratch_shapes=[_table_scratch(), pltpu.VMEM((tb, 2 * TILE_ROWS), F32),
                        pltpu.SemaphoreType.DMA(())],
        compiler_params=_cparams(("arbitrary",)),
        name="peer_v",
    )(idx, act, gate, mask8, grp, grpt, x1, gt2, gfin, tab_v)
    return out


def _mla_head_cols(rope_cols, nope_cols):
    pad = lambda n: [-1] * n
    r1 = list(rope_cols[:16]) if rope_cols is not None else pad(16)
    r2 = list(rope_cols[16:]) if rope_cols is not None else pad(16)
    n1 = list(nope_cols[:48]) if nope_cols is not None else pad(48)
    n2 = list(nope_cols[48:]) if nope_cols is not None else pad(16)
    return r1 + n1 + r2 + n2 + pad(32)


def _take_cols(w, cols):
    cols = np.asarray(cols)
    out = jnp.take(w, jnp.asarray(np.maximum(cols, 0)), axis=1)
    return jnp.where(jnp.asarray(cols >= 0)[None, :], out, 0.0)


def _layer_weights(w_in, w_uq, w_ukv):
    qk = MLA_QK_DIM
    uq_cols, uk_cols, uv_cols = [], [], []
    for h in range(MLA_HEADS):
        uq_cols += _mla_head_cols(range(h * qk + MLA_QK_NOPE, (h + 1) * qk), range(h * qk, h * qk + MLA_QK_NOPE))
        kv0 = h * (MLA_QK_NOPE + MLA_V_DIM)
        uk_cols += _mla_head_cols(None, range(kv0, kv0 + MLA_QK_NOPE))
        uv_cols += list(range(kv0 + MLA_QK_NOPE, kv0 + MLA_QK_NOPE + MLA_V_DIM)) + [-1] * (LANES - MLA_V_DIM)
    o_kr = MLA_Q_RANK + MLA_KV_RANK
    o_r = o_kr + MLA_QK_ROPE
    a_cols = list(range(o_kr)) + _mla_head_cols(range(o_kr, o_r), None)

    def pair_cols(base):
        cols = []
        for p in range(RET_HEADS // 2):
            a, b = base + 2 * p * RET_HEAD_DIM, base + (2 * p + 1) * RET_HEAD_DIM
            cols += list(range(a, a + 32)) + list(range(b, b + 32)) + list(range(a + 32, a + 64)) + list(range(b + 32, b + 64))
        return cols

    r_cols = (pair_cols(o_r) + pair_cols(o_r + RET_WIDTH)
              + list(range(o_r + 2 * RET_WIDTH, o_r + 4 * RET_WIDTH)))
    bf = lambda a: a.astype(BF16)
    return (bf(_take_cols(w_in, a_cols)), bf(_take_cols(w_in, r_cols)), bf(_take_cols(w_uq, uq_cols)),
            bf(_take_cols(w_ukv, uk_cols)), bf(_take_cols(w_ukv, uv_cols)))


def kernel(x, c, positions, w_ada, b_ada, g_norm1, w_in, g_q_norm, w_uq, g_kv_norm, w_ukv, g_ret_norm,
           w_out, g_norm2, w_query, sub_keys, expert_u, expert_v, g_final):
    b, s, d = x.shape
    t = b * s
    depth = w_ada.shape[0]
    tabs = _rope_tables(positions)
    x2 = x.reshape(t, d)
    for l in range(depth):
        mod = _adaln(c, w_ada[l], b_ada[l])
        sh1, sc1, gt1, sh2, sc2, gt2 = [m.reshape(b, 1, d) for m in jnp.split(mod, 6, axis=-1)]
        wa, wr, wuq, wuk, wuv = _layer_weights(w_in[l], w_uq[l], w_ukv[l])
        q, k, v, rq, rk, rv, rg = _proj(
            x2, sc1, sh1, g_norm1[l].reshape(1, d), wa, wr, g_q_norm[l].reshape(1, -1), wuq,
            g_kv_norm[l].reshape(1, -1), wuk, wuv, tabs, s)
        r3 = lambda a: a.reshape(b, s, a.shape[-1])
        y_mla = _attention(r3(q), r3(k), r3(v)).reshape(t, -1)
        y_ret = _retention(r3(rq), r3(rk), r3(rv), r3(rg), g_ret_norm[l]).reshape(t, -1)
        x1, h2, st = _mix(y_mla, y_ret, x2, gt1, sc2, sh2, g_norm2[l].reshape(1, d),
                          w_out[l].astype(BF16), w_query[l].astype(BF16), sub_keys[l].astype(BF16), s)
        eidx4, gate = _topk(st)
        x2 = _peer_apply(h2, eidx4, gate, _pack_table(expert_u[l]), _pack_table(expert_v[l]),
                         x1, gt2, g_final, s, final=(l == depth - 1))
    return x2.reshape(b, s, d)
```
